```python
import math
import jax, jax.numpy as jnp
from jax import lax
import numpy as np

D_MODEL = 1024
BATCH = 16
SEQ = 2048
DEPTH = 1
DEC_BATCH = 128
DEC_SEQ = 4
PAST_LEN = 8192
PAGE_SIZE = 128

N_META = 16
N_HEADS = 8
HEAD_DIM = D_MODEL // (2 * N_HEADS)
ATTN_W = N_HEADS * 2 * HEAD_DIM
LRU_W = D_MODEL
LRU_BLOCKS = 4
LRU_BW = LRU_W // LRU_BLOCKS
CONV_W = 4
LRU_C = 8.0
N_EXPERTS = 32
TOP_K = 4
D_FF = D_MODEL
SWIGLU_LIMIT = 7.0
SWIGLU_ALPHA = 1.702
MOE_BLOCK = 128
Q_BLOCK = 128
N_IN = 3 * ATTN_W + 2 * LRU_W + 2 * D_MODEL
SPLITS = (ATTN_W, 2 * ATTN_W, 3 * ATTN_W, 3 * ATTN_W + LRU_W, 3 * ATTN_W + 2 * LRU_W, 3 * ATTN_W + 2 * LRU_W + D_MODEL)
EPS = 1e-6
SUBLN_EPS = 1e-5

kernel_name = "hybrid_diffattn_rglru_moe_step"


def rmsnorm(x, g, eps=EPS):
    xf = x.astype(jnp.float32)
    y = xf * lax.rsqrt(jnp.mean(xf * xf, axis=-1, keepdims=True) + eps)
    return (y * g.astype(jnp.float32)).astype(x.dtype)


def diff_combine(s1, s2, mask, lam):
    p1 = jax.nn.softmax(jnp.where(mask, s1.astype(jnp.float32), -jnp.inf), axis=-1)
    p2 = jax.nn.softmax(jnp.where(mask, s2.astype(jnp.float32), -jnp.inf), axis=-1)
    return p1 - lam * p2


def diff_attn_prompt(q, k, v, lam):
    B, T = q.shape[0], q.shape[1]
    nqb = -(-T // Q_BLOCK)
    qp = jnp.pad(q, ((0, 0), (0, nqb * Q_BLOCK - T), (0, 0), (0, 0)))
    qb = qp.reshape(B, nqb, Q_BLOCK, N_HEADS, 2 * HEAD_DIM).swapaxes(0, 1)
    scale = HEAD_DIM ** -0.5
    k1, k2 = k[..., :HEAD_DIM], k[..., HEAD_DIM:]
    key_pos = jnp.arange(T)

    def block(args):
        qblk, bi = args
        q_pos = bi * Q_BLOCK + jnp.arange(Q_BLOCK)
        mask = key_pos[None, :] <= q_pos[:, None]
        s1 = jnp.einsum('bqhd,bkhd->bhqk', qblk[..., :HEAD_DIM], k1) * scale
        s2 = jnp.einsum('bqhd,bkhd->bhqk', qblk[..., HEAD_DIM:], k2) * scale
        p = diff_combine(s1, s2, mask, lam)
        return jnp.einsum('bhqk,bkhd->bqhd', p.astype(v.dtype), v)

    o = lax.map(block, (qb, jnp.arange(nqb)))
    return o.swapaxes(0, 1).reshape(B, nqb * Q_BLOCK, N_HEADS, 2 * HEAD_DIM)[:, :T]


def diff_attn_sample(q, k, v, cache_k, cache_v, layer, page_table, lam):
    S = q.shape[1]
    n_past = page_table.shape[1] * cache_k.shape[2]
    scale = HEAD_DIM ** -0.5
    mask = jnp.concatenate([jnp.ones((S, n_past), bool), jnp.tril(jnp.ones((S, S), bool))], axis=1)

    def seq(args):
        qs, kn, vn, pages = args
        kp = cache_k[layer, pages].reshape(n_past, N_HEADS, 2 * HEAD_DIM)
        vp = cache_v[layer, pages].reshape(n_past, N_HEADS, 2 * HEAD_DIM)
        k_all = jnp.concatenate([kp, kn.astype(kp.dtype)], axis=0)
        v_all = jnp.concatenate([vp, vn.astype(vp.dtype)], axis=0)
        s1 = jnp.einsum('qhd,khd->hqk', qs[..., :HEAD_DIM], k_all[..., :HEAD_DIM]) * scale
        s2 = jnp.einsum('qhd,khd->hqk', qs[..., HEAD_DIM:], k_all[..., HEAD_DIM:]) * scale
        p = diff_combine(s1, s2, mask, lam)
        return jnp.einsum('hqk,khd->qhd', p.astype(v_all.dtype), v_all)

    return lax.map(seq, (q, k, v, page_table)).astype(q.dtype)


def causal_conv(x, buf, w, b):
    T = x.shape[1]
    xp = jnp.concatenate([buf.astype(x.dtype), x], axis=1)
    y = b + xp[:, 0:T] * w[0]
    for j in range(1, CONV_W):
        y = y + xp[:, j:j + T] * w[j]
    return y, xp[:, -(CONV_W - 1):]


def rg_lru(x, h0, wa, ba, wx, bx, lam):
    B, T, W = x.shape
    xb = x.reshape(B, T, LRU_BLOCKS, LRU_BW)
    r = jax.nn.sigmoid(jnp.einsum('btnc,ncd->btnd', xb, wa).reshape(B, T, W) + ba)
    i = jax.nn.sigmoid(jnp.einsum('btnc,ncd->btnd', xb, wx).reshape(B, T, W) + bx)
    log_a = (-LRU_C * r.astype(jnp.float32)) * jax.nn.softplus(-lam.astype(jnp.float32))
    a = jnp.exp(log_a)
    mult = jnp.sqrt(-jnp.expm1(2.0 * log_a))
    u = x.astype(jnp.float32) * i.astype(jnp.float32) * mult

    def step(h, au):
        a_t, u_t = au
        h = a_t * h + u_t
        return h, h

    h_last, hs = lax.scan(step, h0.astype(jnp.float32), (a.swapaxes(0, 1), u.swapaxes(0, 1)))
    return hs.swapaxes(0, 1).astype(x.dtype), h_last.astype(h0.dtype)


def token_mixer(u, attn_fn, conv_buf, h0, p, lam_init):
    B, T, _ = u.shape
    proj = u @ p['w_in']
    q, k, v, xr, yr, ga, gr = jnp.split(proj, list(SPLITS), axis=-1)
    q = q.reshape(B, T, N_HEADS, 2 * HEAD_DIM)
    k = k.reshape(B, T, N_HEADS, 2 * HEAD_DIM)
    v = v.reshape(B, T, N_HEADS, 2 * HEAD_DIM)
    lam = (jnp.exp(jnp.sum(p['lq1'].astype(jnp.float32) * p['lk1'].astype(jnp.float32)))
           - jnp.exp(jnp.sum(p['lq2'].astype(jnp.float32) * p['lk2'].astype(jnp.float32))) + lam_init)
    o = attn_fn(q, k, v, lam)
    o = rmsnorm(o, p['subln_g'], SUBLN_EPS) * (1.0 - lam_init)
    attn_out = o.reshape(B, T, ATTN_W) @ p['w_attn_proj']
    xc, new_buf = causal_conv(xr, conv_buf, p['conv_w'], p['conv_b'])
    hs, h_last = rg_lru(xc, h0, p['rg_wa'], p['rg_ba'], p['rg_wx'], p['rg_bx'], p['rg_lambda'])
    lru_out = (hs * jax.nn.gelu(yr, approximate=True)) @ p['w_lru_proj']
    merged = jax.nn.sigmoid(ga) * attn_out + jax.nn.sigmoid(gr) * lru_out
    return merged @ p['w_out'], k, v, new_buf, h_last


def moe(x, p):
    lead = x.shape[:-1]
    xt = x.reshape(-1, D_MODEL)
    N = xt.shape[0]
    logits = (xt @ p['router_w']).astype(jnp.float32) + p['router_b'].astype(jnp.float32)
    top_v, top_i = lax.top_k(logits, TOP_K)
    gates = jax.nn.softmax(top_v, axis=-1)
    flat_e = top_i.reshape(-1).astype(jnp.int32)
    flat_t = jnp.repeat(jnp.arange(N, dtype=jnp.int32), TOP_K)
    flat_g = gates.reshape(-1)
    order = jnp.argsort(flat_e, stable=True)
    se = flat_e[order]
    counts = jnp.zeros((N_EXPERTS,), jnp.int32).at[flat_e].add(1)
    padded = (counts + MOE_BLOCK - 1) // MOE_BLOCK * MOE_BLOCK
    start = jnp.cumsum(counts) - counts
    pend = jnp.cumsum(padded)
    pstart = pend - padded
    dest = pstart[se] + jnp.arange(N * TOP_K, dtype=jnp.int32) - start[se]
    n_rows = -(-(N * TOP_K + N_EXPERTS * (MOE_BLOCK - 1)) // MOE_BLOCK) * MOE_BLOCK
    n_blk = n_rows // MOE_BLOCK
    row_tok = jnp.full((n_rows,), N, jnp.int32).at[dest].set(flat_t[order])
    row_gate = jnp.zeros((n_rows,), jnp.float32).at[dest].set(flat_g[order])
    blk_e = jnp.minimum(jnp.searchsorted(pend, jnp.arange(n_blk, dtype=jnp.int32) * MOE_BLOCK, side='right'), N_EXPERTS - 1)
    x_rows = jnp.concatenate([xt, jnp.zeros((1, D_MODEL), xt.dtype)], axis=0)[row_tok]
    x_rows = x_rows.reshape(n_blk, MOE_BLOCK, D_MODEL)

    def block(args):
        xb, e = args
        g = xb @ p['w_gate'][e] + p['b_gate'][e]
        up = xb @ p['w_up'][e] + p['b_up'][e]
        g = jnp.minimum(g, SWIGLU_LIMIT)
        up = jnp.clip(up, -SWIGLU_LIMIT, SWIGLU_LIMIT)
        h = (up + 1.0) * (g * jax.nn.sigmoid(SWIGLU_ALPHA * g))
        return h @ p['w_down'][e] + p['b_down'][e]

    y_rows = lax.map(block, (x_rows, blk_e)).reshape(n_rows, D_MODEL)
    y = jax.ops.segment_sum(y_rows * row_gate[:, None].astype(y_rows.dtype), row_tok, num_segments=N + 1)[:N]
    return y.astype(x.dtype).reshape(lead + (D_MODEL,))


def setup_inputs(seed: int = 0) -> dict:
    key = jax.random.key(seed)
    ks = iter(list(jax.random.split(key, 40)))
    f32 = jnp.float32
    n_pages = PAST_LEN // PAGE_SIZE
    n_pool = (DEC_BATCH * n_pages * 5) // 4

    def nrm(shape, scale=1.0):
        return scale * jax.random.normal(next(ks), shape, f32)

    def gain(shape):
        return 1.0 + 0.01 * jax.random.normal(next(ks), shape, f32)

    x_prompt = nrm((BATCH, SEQ, D_MODEL))
    x_sample = nrm((DEC_BATCH, DEC_SEQ, D_MODEL))
    cache_k = nrm((DEPTH, n_pool, PAGE_SIZE, N_HEADS, 2 * HEAD_DIM))
    cache_v = nrm((DEPTH, n_pool, PAGE_SIZE, N_HEADS, 2 * HEAD_DIM))
    state_conv = nrm((DEPTH, DEC_BATCH, CONV_W - 1, LRU_W))
    state_h = nrm((DEPTH, DEC_BATCH, LRU_W), 0.5)
    page_table = jax.random.permutation(next(ks), n_pool)[: DEC_BATCH * n_pages].reshape(DEC_BATCH, n_pages).astype(jnp.int32)
    meta_tokens = nrm((N_META, D_MODEL))
    norm1_g = gain((DEPTH, D_MODEL))
    w_in = nrm((DEPTH, D_MODEL, N_IN), D_MODEL ** -0.5)
    lambda_q1 = nrm((DEPTH, HEAD_DIM), 0.1)
    lambda_k1 = nrm((DEPTH, HEAD_DIM), 0.1)
    lambda_q2 = nrm((DEPTH, HEAD_DIM), 0.1)
    lambda_k2 = nrm((DEPTH, HEAD_DIM), 0.1)
    subln_g = gain((DEPTH, 2 * HEAD_DIM))
    conv_w = nrm((DEPTH, CONV_W, LRU_W), CONV_W ** -0.5)
    conv_b = nrm((DEPTH, LRU_W), 0.01)
    rg_wa = nrm((DEPTH, LRU_BLOCKS, LRU_BW, LRU_BW), LRU_BW ** -0.5)
    rg_ba = nrm((DEPTH, LRU_W), 0.01)
    rg_wx = nrm((DEPTH, LRU_BLOCKS, LRU_BW, LRU_BW), LRU_BW ** -0.5)
    rg_bx = nrm((DEPTH, LRU_W), 0.01)
    a0 = jax.random.uniform(next(ks), (DEPTH, LRU_W), f32, 0.9, 0.999)
    s0 = a0 ** (1.0 / LRU_C)
    rg_lambda = jnp.log(s0) - jnp.log1p(-s0)
    w_attn_proj = nrm((DEPTH, ATTN_W, D_MODEL), ATTN_W ** -0.5)
    w_lru_proj = nrm((DEPTH, LRU_W, D_MODEL), LRU_W ** -0.5)
    w_out = nrm((DEPTH, D_MODEL, D_MODEL), D_MODEL ** -0.5)
    norm2_g = gain((DEPTH, D_MODEL))
    router_w = nrm((DEPTH, D_MODEL, N_EXPERTS), D_MODEL ** -0.5)
    router_b = nrm((DEPTH, N_EXPERTS), 0.01)
    w_gate = nrm((DEPTH, N_EXPERTS, D_MODEL, D_FF), D_MODEL ** -0.5)
    b_gate = nrm((DEPTH, N_EXPERTS, D_FF), 0.01)
    w_up = nrm((DEPTH, N_EXPERTS, D_MODEL, D_FF), D_MODEL ** -0.5)
    b_up = nrm((DEPTH, N_EXPERTS, D_FF), 0.01)
    w_down = nrm((DEPTH, N_EXPERTS, D_FF, D_MODEL), D_FF ** -0.5)
    b_down = nrm((DEPTH, N_EXPERTS, D_MODEL), 0.01)
    final_norm_g = gain((D_MODEL,))
    return {"x_prompt": x_prompt, "x_sample": x_sample, "cache_k": cache_k, "cache_v": cache_v,
            "state_conv": state_conv, "state_h": state_h, "page_table": page_table,
            "meta_tokens": meta_tokens, "norm1_g": norm1_g, "w_in": w_in,
            "lambda_q1": lambda_q1, "lambda_k1": lambda_k1, "lambda_q2": lambda_q2, "lambda_k2": lambda_k2,
            "subln_g": subln_g, "conv_w": conv_w, "conv_b": conv_b,
            "rg_wa": rg_wa, "rg_ba": rg_ba, "rg_wx": rg_wx, "rg_bx": rg_bx, "rg_lambda": rg_lambda,
            "w_attn_proj": w_attn_proj, "w_lru_proj": w_lru_proj, "w_out": w_out, "norm2_g": norm2_g,
            "router_w": router_w, "router_b": router_b, "w_gate": w_gate, "b_gate": b_gate,
            "w_up": w_up, "b_up": b_up, "w_down": w_down, "b_down": b_down, "final_norm_g": final_norm_g}


def reference(x_prompt, x_sample, cache_k, cache_v, state_conv, state_h, page_table, meta_tokens, norm1_g, w_in,
              lambda_q1, lambda_k1, lambda_q2, lambda_k2, subln_g, conv_w, conv_b, rg_wa, rg_ba, rg_wx, rg_bx,
              rg_lambda, w_attn_proj, w_lru_proj, w_out, norm2_g, router_w, router_b, w_gate, b_gate, w_up, b_up,
              w_down, b_down, final_norm_g):
    B = x_prompt.shape[0]
    hp = jnp.concatenate([jnp.broadcast_to(meta_tokens.astype(x_prompt.dtype)[None], (B, N_META, D_MODEL)), x_prompt], axis=1)
    hs = x_sample
    kp_l, vp_l, cp_l, hp_l = [], [], [], []
    ks_l, vs_l, cs_l, hs_l = [], [], [], []
    for l in range(DEPTH):
        p = dict(w_in=w_in[l], lq1=lambda_q1[l], lk1=lambda_k1[l], lq2=lambda_q2[l], lk2=lambda_k2[l],
                 subln_g=subln_g[l], conv_w=conv_w[l], conv_b=conv_b[l], rg_wa=rg_wa[l], rg_ba=rg_ba[l],
                 rg_wx=rg_wx[l], rg_bx=rg_bx[l], rg_lambda=rg_lambda[l], w_attn_proj=w_attn_proj[l],
                 w_lru_proj=w_lru_proj[l], w_out=w_out[l], router_w=router_w[l], router_b=router_b[l],
                 w_gate=w_gate[l], b_gate=b_gate[l], w_up=w_up[l], b_up=b_up[l], w_down=w_down[l], b_down=b_down[l])
        lam_init = 0.8 - 0.6 * math.exp(-0.3 * l)
        buf0 = jnp.zeros((B, CONV_W - 1, LRU_W), hp.dtype)
        h0 = jnp.zeros((B, LRU_W), state_h.dtype)
        mix, k, v, cb, hl = token_mixer(rmsnorm(hp, norm1_g[l]), diff_attn_prompt, buf0, h0, p, lam_init)
        hp = hp + mix
        kp_l.append(k); vp_l.append(v); cp_l.append(cb); hp_l.append(hl)
        if l == DEPTH - 1:
            hp = hp[:, N_META:]
        hp = hp + moe(rmsnorm(hp, norm2_g[l]), p)
        attn_s = lambda q, kk, vv, lam, l=l: diff_attn_sample(q, kk, vv, cache_k, cache_v, l, page_table, lam)
        mix, k, v, cb, hl = token_mixer(rmsnorm(hs, norm1_g[l]), attn_s, state_conv[l], state_h[l], p, lam_init)
        hs = hs + mix
        ks_l.append(k); vs_l.append(v); cs_l.append(cb); hs_l.append(hl)
        hs = hs + moe(rmsnorm(hs, norm2_g[l]), p)
    y_prompt = rmsnorm(hp, final_norm_g)
    y_sample = rmsnorm(hs, final_norm_g)
    return (y_prompt, y_sample, jnp.stack(kp_l), jnp.stack(vp_l), jnp.stack(cp_l), jnp.stack(hp_l),
            jnp.stack(ks_l), jnp.stack(vs_l), jnp.stack(cs_l), jnp.stack(hs_l))
```

```python
import functools
import math

import jax
import jax.numpy as jnp
from jax import lax
from jax.experimental import pallas as pl
from jax.experimental.pallas import tpu as pltpu

F32 = jnp.float32
BF16 = jnp.bfloat16
I32 = jnp.int32

LANES = 128
SUBLANES = 8
N_HEADS = 8
HEAD_W = 128
MAP_W = 64
N_META = 16
CONV_W = 4
LRU_BLOCKS = 4
LRU_C = 8.0
N_EXPERTS = 32
TOP_K = 4
SWIGLU_LIMIT = 7.0
SWIGLU_ALPHA = 1.702
EPS = 1e-6
SUBLN_EPS = 1e-5
LAM_INIT = 0.8 - 0.6 * math.exp(-0.3 * 0)
NEG_BIG = -1e30
VMEM_LIMIT = 56 * 1024 * 1024

ROW_TILE = 256
MOE_TILE = 256
ATT_TILE = 256
LRU_TT = 64
LRU_GB = 8
PAGES_PER_STEP = 4
DISPATCH_TILE = 512
COMBINE_TILE = 128


def _cparams(sem):
    return pltpu.CompilerParams(dimension_semantics=sem, vmem_limit_bytes=VMEM_LIMIT)


def _sigmoid(x):
    return 1.0 / (1.0 + jnp.exp(-x))


def _expm1(x):
    u = jnp.exp(x)
    um1 = u - 1.0
    return jnp.where(u == 1.0, x, jnp.where(um1 == -1.0, -1.0, um1 * x / jnp.log(u)))


def _lam_from(lv):
    a = jnp.sum(lv[0:1, :] * lv[1:2, :], axis=-1, keepdims=True)
    b = jnp.sum(lv[2:3, :] * lv[3:4, :], axis=-1, keepdims=True)
    return jnp.exp(a) - jnp.exp(b) + LAM_INIT


def _subln(o, g):
    ms = jnp.mean(o * o, axis=-1, keepdims=True)
    return ((o * lax.rsqrt(ms + SUBLN_EPS)) * g) * (1.0 - LAM_INIT)


def _inproj_kernel(x_ref, g_ref, w_ref, qb_ref, kb_ref, vb_ref, k3_ref, v3_ref,
                   xr_ref, yr_ref, ga_ref, gr_ref):
    tm, d = x_ref.shape
    x = x_ref[...]
    ms = jnp.mean(x * x, axis=-1, keepdims=True)
    ub = ((x * lax.rsqrt(ms + EPS)) * g_ref[...]).astype(BF16)

    def proj(j):
        return jnp.dot(ub, w_ref[:, j * d:(j + 1) * d], preferred_element_type=F32)

    def store_heads(ref, val):
        for s in range(N_HEADS):
            ref[pl.ds(s, tm, stride=N_HEADS), :] = val[:, s * HEAD_W:(s + 1) * HEAD_W]

    q = proj(0)
    qb_ref[...] = (q * (MAP_W ** -0.5)).astype(BF16)
    k = proj(1)
    kb_ref[...] = k.astype(BF16)
    store_heads(k3_ref, k)
    v = proj(2)
    vb_ref[...] = v.astype(BF16)
    store_heads(v3_ref, v)
    xr_ref[...] = proj(3)
    yr_ref[...] = proj(4)
    ga_ref[...] = proj(5)
    gr_ref[...] = proj(6)


def _inproj(x, g, w_bf, tm):
    r, d = x.shape
    assert r % tm == 0
    row = lambda i: (i, 0)
    f32o = jax.ShapeDtypeStruct((r, d), F32)
    bfo = jax.ShapeDtypeStruct((r, d), BF16)
    h3o = jax.ShapeDtypeStruct((r * N_HEADS, HEAD_W), F32)
    blk = pl.BlockSpec((tm, d), row)
    blk3 = pl.BlockSpec((tm * N_HEADS, HEAD_W), row)
    return pl.pallas_call(
        _inproj_kernel,
        grid=(r // tm,),
        in_specs=[blk, pl.BlockSpec((1, d), lambda i: (0, 0)),
                  pl.BlockSpec(w_bf.shape, lambda i: (0, 0), pipeline_mode=pl.Buffered(1))],
        out_specs=[blk, blk, blk, blk3, blk3, blk, blk, blk, blk],
        out_shape=[bfo, bfo, bfo, h3o, h3o, f32o, f32o, f32o, f32o],
        compiler_params=_cparams(("parallel",)),
        name="inproj",
    )(x, g, w_bf)


def _attn_prompt_kernel(lv_ref, q_ref, k_ref, v_ref, km_ref, vm_ref, g_ref, o_ref, *, tq):
    qi = pl.program_id(2)
    lam = _lam_from(lv_ref[...])
    q = q_ref[0]
    lane = lax.broadcasted_iota(I32, q.shape, 1)
    zero = jnp.zeros_like(q)
    qq = jnp.concatenate([jnp.where(lane < MAP_W, q, zero), jnp.where(lane >= MAP_W, q, zero)], axis=0)

    def scores(kc):
        return lax.dot_general(qq, kc, (((1,), (1,)), ((), ())), preferred_element_type=F32)

    s0 = scores(km_ref[...])
    m = jnp.max(s0, axis=-1, keepdims=True)
    p0 = jnp.exp(s0 - m)
    l = jnp.sum(p0, axis=-1, keepdims=True)
    acc = jnp.dot(p0.astype(BF16), vm_ref[...], preferred_element_type=F32)

    def update(carry, s, vc):
        m, l, acc = carry
        m_new = jnp.maximum(m, jnp.max(s, axis=-1, keepdims=True))
        alpha = jnp.exp(m - m_new)
        p = jnp.exp(s - m_new)
        l = alpha * l + jnp.sum(p, axis=-1, keepdims=True)
        acc = alpha * acc + jnp.dot(p.astype(BF16), vc, preferred_element_type=F32)
        return m_new, l, acc

    def body(j, carry):
        off = pl.multiple_of(j * tq, tq)
        return update(carry, scores(k_ref[0, pl.ds(off, tq), :]), v_ref[0, pl.ds(off, tq), :])

    carry = lax.fori_loop(0, qi, body, (m, l, acc))

    off = pl.multiple_of(qi * tq, tq)
    s = scores(k_ref[0, pl.ds(off, tq), :])
    row = lax.broadcasted_iota(I32, s.shape, 0)
    col = lax.broadcasted_iota(I32, s.shape, 1)
    rowq = jnp.where(row >= tq, row - tq, row)
    s = jnp.where(col <= rowq, s, -jnp.inf)
    m, l, acc = update(carry, s, v_ref[0, pl.ds(off, tq), :])

    o = acc / l
    o = o[:tq] - lam * o[tq:]
    o_ref[0] = _subln(o, g_ref[...]).astype(o_ref.dtype)


def _attn_prompt(lv, qb, kb, vb, kmb, vmb, subln_g):
    b, t, d = qb.shape
    tq = min(ATT_TILE, t)
    assert t % tq == 0
    kern = functools.partial(_attn_prompt_kernel, tq=tq)
    return pl.pallas_call(
        kern,
        grid=(b, N_HEADS, t // tq),
        in_specs=[pl.BlockSpec(lv.shape, lambda bi, h, i: (0, 0)),
                  pl.BlockSpec((1, tq, HEAD_W), lambda bi, h, i: (bi, i, h)),
                  pl.BlockSpec((1, t, HEAD_W), lambda bi, h, i: (bi, 0, h)),
                  pl.BlockSpec((1, t, HEAD_W), lambda bi, h, i: (bi, 0, h)),
                  pl.BlockSpec((N_META, HEAD_W), lambda bi, h, i: (0, h)),
                  pl.BlockSpec((N_META, HEAD_W), lambda bi, h, i: (0, h)),
                  pl.BlockSpec((1, HEAD_W), lambda bi, h, i: (0, 0))],
        out_specs=pl.BlockSpec((1, tq, HEAD_W), lambda bi, h, i: (bi, i, h)),
        out_shape=jax.ShapeDtypeStruct((b, t, d), BF16),
        compiler_params=_cparams(("parallel", "parallel", "arbitrary")),
        name="attn_prompt",
    )(lv, qb, kb, vb, kmb, vmb, subln_g)


def _attn_sample_kernel(pt_ref, lv_ref, q_ref, kn_ref, vn_ref, g_ref, *rest, pp, s_new):
    kpages = rest[:pp]
    vpages = rest[pp:2 * pp]
    o_ref = rest[2 * pp]
    m_scr, l_scr, acc_scr = rest[2 * pp + 1:]
    g = pl.program_id(1)
    ng = pl.num_programs(1)
    rows = 2 * s_new

    @pl.when(g == 0)
    def _():
        m_scr[...] = jnp.full(m_scr.shape, -jnp.inf, F32)
        l_scr[...] = jnp.zeros(l_scr.shape, F32)
        acc_scr[...] = jnp.zeros(acc_scr.shape, F32)

    lane = lax.broadcasted_iota(I32, (s_new, HEAD_W), 1)

    def qq_of(h):
        qh = q_ref[0, :, h * HEAD_W:(h + 1) * HEAD_W]
        return jnp.concatenate([jnp.where(lane < MAP_W, qh, 0.0),
                                jnp.where(lane >= MAP_W, qh, 0.0)], axis=0).astype(BF16)

    def scores(qq, kc):
        return lax.dot_general(qq, kc, (((1,), (1,)), ((), ())), preferred_element_type=F32)

    def update(h, s_list, v_list):
        m_prev = m_scr[h]
        s_max = s_list[0]
        for s in s_list[1:]:
            s_max = jnp.maximum(s_max, s)
        m_new = jnp.maximum(m_prev, jnp.max(s_max, axis=-1, keepdims=True))
        alpha = jnp.exp(m_prev - m_new)
        l_add = None
        pv = None
        for s, vc in zip(s_list, v_list):
            p = jnp.exp(s - m_new[:, :s.shape[1]])
            ps = jnp.sum(p, axis=-1, keepdims=True)
            l_add = ps if l_add is None else l_add + ps
            d = jnp.dot(p.astype(BF16), vc, preferred_element_type=F32)
            pv = d if pv is None else pv + d
        m_scr[h] = m_new
        l_scr[h] = alpha * l_scr[h] + l_add
        acc_scr[h] = alpha * acc_scr[h] + pv

    for h in range(N_HEADS):
        qq = qq_of(h)
        s_list = [scores(qq, kp[:, h, :].astype(BF16)) for kp in kpages]
        v_list = [vp[:, h, :].astype(BF16) for vp in vpages]
        update(h, s_list, v_list)

    @pl.when(g == ng - 1)
    def _():
        lam = _lam_from(lv_ref[...])
        pad = jnp.zeros((2 * SUBLANES - s_new, HEAD_W), F32)
        row = lax.broadcasted_iota(I32, (rows, 2 * SUBLANES), 0)
        col = lax.broadcasted_iota(I32, (rows, 2 * SUBLANES), 1)
        rowq = jnp.where(row >= s_new, row - s_new, row)
        causal = col <= rowq
        for h in range(N_HEADS):
            sl = slice(h * HEAD_W, (h + 1) * HEAD_W)
            kn = jnp.concatenate([kn_ref[0, :, sl], pad], axis=0).astype(BF16)
            vn = jnp.concatenate([vn_ref[0, :, sl], pad], axis=0).astype(BF16)
            s = jnp.where(causal, scores(qq_of(h), kn), -jnp.inf)
            update(h, [s], [vn])
            o = acc_scr[h] / l_scr[h]
            o = o[:s_new] - lam * o[s_new:]
            o_ref[0, :, sl] = _subln(o, g_ref[...])


def _attn_sample(page_table, lv, q, kn, vn, subln_g, cache_k, cache_v):
    bd, s_new, d = q.shape
    n_pages = page_table.shape[1]
    page = cache_k.shape[1]
    pp = PAGES_PER_STEP if n_pages % PAGES_PER_STEP == 0 else 1
    rows = 2 * s_new
    pt = page_table.reshape(-1)
    seq = lambda b, g, pt: (b, 0, 0)

    def page_spec(p):
        return pl.BlockSpec((None, page, N_HEADS, HEAD_W),
                            lambda b, g, pt, p=p: (pt[b * n_pages + g * pp + p], 0, 0, 0))

    grid_spec = pltpu.PrefetchScalarGridSpec(
        num_scalar_prefetch=1,
        grid=(bd, n_pages // pp),
        in_specs=[pl.BlockSpec(lv.shape, lambda b, g, pt: (0, 0)),
                  pl.BlockSpec((1, s_new, d), seq), pl.BlockSpec((1, s_new, d), seq),
                  pl.BlockSpec((1, s_new, d), seq),
                  pl.BlockSpec((1, HEAD_W), lambda b, g, pt: (0, 0))]
                 + [page_spec(p) for p in range(pp)] + [page_spec(p) for p in range(pp)],
        out_specs=pl.BlockSpec((1, s_new, d), seq),
        scratch_shapes=[pltpu.VMEM((N_HEADS, rows, LANES), F32),
                        pltpu.VMEM((N_HEADS, rows, LANES), F32),
                        pltpu.VMEM((N_HEADS, rows, HEAD_W), F32)],
    )
    kern = functools.partial(_attn_sample_kernel, pp=pp, s_new=s_new)
    return pl.pallas_call(
        kern,
        grid_spec=grid_spec,
        out_shape=jax.ShapeDtypeStruct((bd, s_new, d), F32),
        compiler_params=_cparams(("parallel", "arbitrary")),
        name="attn_sample",
    )(pt, lv, q, kn, vn, subln_g, *([cache_k] * pp), *([cache_v] * pp))


def _lru_kernel(xr_ref, yr_ref, cb0_ref, h0_ref, cw_ref, cb_ref, wa_ref, ba_ref, wx_ref, bx_ref,
                lam_ref, hg_ref, hl_ref, xbuf, a_scr, u_scr, h_scr, *, n_steps):
    gb, tt, w = xr_ref.shape
    bw = w // LRU_BLOCKS
    i = pl.program_id(1)
    halo = SUBLANES

    @pl.when(i == 0)
    def _():
        xbuf[:, halo - (CONV_W - 1):halo, :] = cb0_ref[...]
        h_scr[...] = h0_ref[...]

    xbuf[:, halo:halo + tt, :] = xr_ref[...]
    xc = cb_ref[...] + xbuf[:, halo - 3:halo - 3 + tt, :] * cw_ref[0:1, :]
    for j in range(1, CONV_W):
        xc = xc + xbuf[:, halo - 3 + j:halo - 3 + j + tt, :] * cw_ref[j:j + 1, :]
    xbuf[:, halo - (CONV_W - 1):halo, :] = xbuf[:, halo + tt - (CONV_W - 1):halo + tt, :]

    xc2 = xc.reshape(gb * tt, w)
    xcb = xc2.astype(BF16)
    r_parts, i_parts = [], []
    for c in range(LRU_BLOCKS):
        xs = xcb[:, c * bw:(c + 1) * bw]
        r_parts.append(jnp.dot(xs, wa_ref[c], preferred_element_type=F32))
        i_parts.append(jnp.dot(xs, wx_ref[c], preferred_element_type=F32))
    r = _sigmoid(jnp.concatenate(r_parts, axis=-1) + ba_ref[...])
    ig = _sigmoid(jnp.concatenate(i_parts, axis=-1) + bx_ref[...])
    nl = -lam_ref[...]
    softplus = jnp.maximum(nl, 0.0) + jnp.log1p(jnp.exp(-jnp.abs(nl)))
    log_a = (-LRU_C * r) * softplus
    a_scr[...] = jnp.exp(log_a).reshape(gb, tt, w)
    mult = jnp.sqrt(-_expm1(2.0 * log_a))
    u_scr[...] = (xc2 * ig * mult).reshape(gb, tt, w)

    def step(t, h):
        h = a_scr[:, t, :] * h + u_scr[:, t, :]
        u_scr[:, t, :] = h
        return h

    h = lax.fori_loop(0, n_steps, step, h_scr[...])
    h_scr[...] = h
    hl_ref[...] = h
    y = yr_ref[...]
    gelu = y * (0.5 * (1.0 + jnp.tanh(math.sqrt(2.0 / math.pi) * (y + 0.044715 * (y * y * y)))))
    hg_ref[...] = (u_scr[...] * gelu).astype(hg_ref.dtype)


def _lru(xr, yr, cb0, h0, p, tt, n_steps):
    b, t, w = xr.shape
    gb = LRU_GB
    assert b % gb == 0 and t % tt == 0 and (n_steps == tt or t == tt)
    bw = w // LRU_BLOCKS
    blk = pl.BlockSpec((gb, tt, w), lambda g, i: (g, i, 0))
    vec = pl.BlockSpec((1, w), lambda g, i: (0, 0))
    wsp = pl.BlockSpec((LRU_BLOCKS, bw, bw), lambda g, i: (0, 0, 0))
    kern = functools.partial(_lru_kernel, n_steps=n_steps)
    return pl.pallas_call(
        kern,
        grid=(b // gb, t // tt),
        in_specs=[blk, blk,
                  pl.BlockSpec((gb, CONV_W - 1, w), lambda g, i: (g, 0, 0)),
                  pl.BlockSpec((gb, w), lambda g, i: (g, 0)),
                  pl.BlockSpec((CONV_W, w), lambda g, i: (0, 0)), vec, wsp, vec, wsp, vec, vec],
        out_specs=[blk, pl.BlockSpec((gb, w), lambda g, i: (g, 0))],
        out_shape=[jax.ShapeDtypeStruct((b, t, w), BF16), jax.ShapeDtypeStruct((b, w), F32)],
        scratch_shapes=[pltpu.VMEM((gb, tt + SUBLANES, w), F32), pltpu.VMEM((gb, tt, w), F32),
                        pltpu.VMEM((gb, tt, w), F32), pltpu.VMEM((gb, w), F32)],
        compiler_params=_cparams(("parallel", "arbitrary")),
        name="lru",
    )(xr, yr, cb0, h0, p["conv_w"], p["conv_b"], p["wa"], p["ba"], p["wx"], p["bx"], p["lam"])


def _post_kernel(o_ref, hg_ref, ga_ref, gr_ref, x_ref, wap_ref, wlp_ref, wo_ref, g2_ref, rw_ref,
                 rb_ref, cin_ref, h1_ref, xn_ref, ei_ref, gt_ref, rk_ref, cnt_ref, carry):
    tm = x_ref.shape[0]

    @pl.when(pl.program_id(0) == 0)
    def _():
        carry[...] = cin_ref[...]

    attn_out = jnp.dot(o_ref[...], wap_ref[...], preferred_element_type=F32)
    lru_out = jnp.dot(hg_ref[...], wlp_ref[...], preferred_element_type=F32)
    merged = _sigmoid(ga_ref[...]) * attn_out + _sigmoid(gr_ref[...]) * lru_out
    h1 = x_ref[...] + jnp.dot(merged.astype(BF16), wo_ref[...], preferred_element_type=F32)
    h1_ref[...] = h1
    ms = jnp.mean(h1 * h1, axis=-1, keepdims=True)
    xn = (h1 * lax.rsqrt(ms + EPS)) * g2_ref[...]
    xn_ref[...] = xn

    lane = lax.broadcasted_iota(I32, (tm, LANES), 1)
    logits = jnp.dot(xn.astype(BF16), rw_ref[...], preferred_element_type=F32) + rb_ref[...]
    lg = jnp.where(lane < N_EXPERTS, logits, NEG_BIG)
    tops, idxs, hots = [], [], []
    for _ in range(TOP_K):
        mx = jnp.max(lg, axis=-1, keepdims=True)
        idx = jnp.min(jnp.where(lg == mx, lane, LANES), axis=-1, keepdims=True)
        hot = lane == idx
        tops.append(mx); idxs.append(idx); hots.append(hot)
        lg = jnp.where(hot, NEG_BIG, lg)
    exps = [jnp.exp(tv - tops[0]) for tv in tops]
    den = exps[0] + exps[1] + exps[2] + exps[3]

    sel = jnp.zeros((tm, LANES), F32)
    for hot in hots:
        sel = sel + jnp.where(hot, 1.0, 0.0)
    r_i = lax.broadcasted_iota(I32, (tm, tm), 0)
    c_i = lax.broadcasted_iota(I32, (tm, tm), 1)
    ltri = jnp.where(c_i < r_i, 1.0, 0.0).astype(BF16)
    base = jnp.dot(ltri, sel.astype(BF16), preferred_element_type=F32) + carry[...]
    new_carry = carry[...] + jnp.sum(sel, axis=0, keepdims=True)
    carry[...] = new_carry
    cnt_ref[...] = new_carry

    ei = jnp.zeros((tm, LANES), I32)
    gt = jnp.zeros((tm, LANES), F32)
    rk = jnp.zeros((tm, LANES), I32)
    for k in range(TOP_K):
        rank = jnp.sum(jnp.where(hots[k], base, 0.0), axis=-1, keepdims=True)
        ei = jnp.where(lane == k, idxs[k], ei)
        gt = jnp.where(lane == k, exps[k] / den, gt)
        rk = jnp.where(lane == k, rank.astype(I32), rk)
    ei_ref[...] = ei
    gt_ref[...] = gt
    rk_ref[...] = rk


def _post(o, hg, ga, gr, x, p, cnt_in, tm):
    r, d = x.shape
    assert r % tm == 0
    row = lambda i: (i, 0)
    fix = lambda i: (0, 0)
    blk = pl.BlockSpec((tm, d), row)
    wsp = pl.BlockSpec((d, d), fix, pipeline_mode=pl.Buffered(1))
    lsp = pl.BlockSpec((tm, LANES), row)
    vec = pl.BlockSpec((1, LANES), fix)
    return pl.pallas_call(
        _post_kernel,
        grid=(r // tm,),
        in_specs=[blk, blk, blk, blk, blk, wsp, wsp, wsp, pl.BlockSpec((1, d), fix),
                  pl.BlockSpec((d, LANES), fix), vec, vec],
        out_specs=[blk, blk, lsp, lsp, lsp, vec],
        out_shape=[jax.ShapeDtypeStruct((r, d), F32), jax.ShapeDtypeStruct((r, d), F32),
                   jax.ShapeDtypeStruct((r, LANES), I32), jax.ShapeDtypeStruct((r, LANES), F32),
                   jax.ShapeDtypeStruct((r, LANES), I32), jax.ShapeDtypeStruct((1, LANES), F32)],
        scratch_shapes=[pltpu.VMEM((1, LANES), F32)],
        compiler_params=_cparams(("arbitrary",)),
        name="post_router",
    )(o, hg, ga, gr, x, p["w_attn_proj"], p["w_lru_proj"], p["w_out"], p["norm2_g"],
      p["router_w"], p["router_b"], cnt_in)


def _dispatch_kernel(e_ref, rk_ref, ps_ref, xn_ref, rows_in_ref, rows_ref, sem, *, tm):
    del rows_in_ref
    base = pl.program_id(0) * tm

    def row_copy(t, d):
        return pltpu.make_async_copy(xn_ref.at[pl.ds(t, 1), :], rows_ref.at[pl.ds(d, 1), :], sem)

    def issue(j, c):
        for k in range(TOP_K):
            n = j * TOP_K + k
            row_copy(base + j, ps_ref[e_ref[n]] + rk_ref[n]).start()
        return c

    lax.fori_loop(0, tm, issue, 0)

    def drain(j, c):
        for k in range(TOP_K):
            row_copy(0, 0).wait()
        return c

    lax.fori_loop(0, tm, drain, 0)


def _dispatch(e_flat, rk_flat, pstart, xn, rows, tm):
    n, d = xn.shape
    assert n % tm == 0
    smem = lambda shape, imap: pl.BlockSpec(shape, imap, memory_space=pltpu.SMEM)
    kern = functools.partial(_dispatch_kernel, tm=tm)
    return pl.pallas_call(
        kern,
        grid=(n // tm,),
        in_specs=[smem((tm * TOP_K,), lambda i: (i,)), smem((tm * TOP_K,), lambda i: (i,)),
                  smem(pstart.shape, lambda i: (0,)),
                  pl.BlockSpec(memory_space=pl.ANY), pl.BlockSpec(memory_space=pl.ANY)],
        out_specs=pl.BlockSpec(memory_space=pl.ANY),
        out_shape=jax.ShapeDtypeStruct(rows.shape, rows.dtype),
        scratch_shapes=[pltpu.SemaphoreType.DMA(())],
        input_output_aliases={4: 0},
        compiler_params=_cparams(("arbitrary",)),
        name="moe_dispatch",
    )(e_flat, rk_flat, pstart, xn, rows)


def _expert_kernel(be_ref, nu_ref, x_ref, wg_ref, bg_ref, wu_ref, bu_ref, wd_ref, bd_ref, y_ref,
                   wg_s, wu_s, wd_s):
    j = pl.program_id(0)
    prev = be_ref[jnp.maximum(j - 1, 0)]

    @pl.when((j == 0) | (be_ref[j] != prev))
    def _():
        wg_s[...] = wg_ref[0].astype(BF16)
        wu_s[...] = wu_ref[0].astype(BF16)
        wd_s[...] = wd_ref[0].astype(BF16)

    @pl.when(j < nu_ref[0])
    def _():
        xb = x_ref[...].astype(BF16)
        g = jnp.dot(xb, wg_s[...], preferred_element_type=F32) + bg_ref[0]
        up = jnp.dot(xb, wu_s[...], preferred_element_type=F32) + bu_ref[0]
        g = jnp.minimum(g, SWIGLU_LIMIT)
        up = jnp.clip(up, -SWIGLU_LIMIT, SWIGLU_LIMIT)
        h = (up + 1.0) * (g * _sigmoid(SWIGLU_ALPHA * g))
        y_ref[...] = jnp.dot(h.astype(BF16), wd_s[...], preferred_element_type=F32) + bd_ref[0]


def _experts(blk_e, n_used, rows, p, tm):
    n_rows, d = rows.shape
    dff = p["w_gate"].shape[2]
    n_blk = n_rows // tm
    wsp = lambda shape: pl.BlockSpec((1,) + shape, lambda j, be, nu: (be[j], 0, 0))
    grid_spec = pltpu.PrefetchScalarGridSpec(
        num_scalar_prefetch=2,
        grid=(n_blk,),
        in_specs=[pl.BlockSpec((tm, d), lambda j, be, nu: (j, 0)),
                  wsp((d, dff)), wsp((1, dff)), wsp((d, dff)), wsp((1, dff)),
                  wsp((dff, d)), wsp((1, d))],
        out_specs=pl.BlockSpec((tm, d), lambda j, be, nu: (j, 0)),
        scratch_shapes=[pltpu.VMEM((d, dff), BF16), pltpu.VMEM((d, dff), BF16),
                        pltpu.VMEM((dff, d), BF16)],
    )
    return pl.pallas_call(
        _expert_kernel,
        grid_spec=grid_spec,
        out_shape=jax.ShapeDtypeStruct((n_rows, d), F32),
        compiler_params=_cparams(("arbitrary",)),
        name="moe_experts",
    )(blk_e, n_used, rows, p["w_gate"], p["b_gate"], p["w_up"], p["b_up"], p["w_down"], p["b_down"])


def _combine_kernel(e_ref, rk_ref, ps_ref, y_ref, gt_ref, h1_ref, gf_ref, out_ref, buf, sem, *, tm):
    def row_copy(d, k, j):
        return pltpu.make_async_copy(y_ref.at[pl.ds(d, 1), :], buf.at[k, pl.ds(j, 1), :], sem)

    def issue(j, c):
        for k in range(TOP_K):
            n = j * TOP_K + k
            row_copy(ps_ref[e_ref[n]] + rk_ref[n], k, j).start()
        return c

    lax.fori_loop(0, tm, issue, 0)

    def drain(j, c):
        for k in range(TOP_K):
            row_copy(0, k, j).wait()
        return c

    lax.fori_loop(0, tm, drain, 0)

    gt = gt_ref[...]
    y = buf[0] * gt[:, 0:1]
    for k in range(1, TOP_K):
        y = y + buf[k] * gt[:, k:k + 1]
    h2 = h1_ref[...] + y
    ms = jnp.mean(h2 * h2, axis=-1, keepdims=True)
    out_ref[...] = (h2 * lax.rsqrt(ms + EPS)) * gf_ref[...]


def _combine(e_flat, rk_flat, pstart, y_rows, gt, h1, gf, tm):
    n, d = h1.shape
    assert n % tm == 0
    smem = lambda shape, imap: pl.BlockSpec(shape, imap, memory_space=pltpu.SMEM)
    row = lambda i: (i, 0)
    kern = functools.partial(_combine_kernel, tm=tm)
    return pl.pallas_call(
        kern,
        grid=(n // tm,),
        in_specs=[smem((tm * TOP_K,), lambda i: (i,)), smem((tm * TOP_K,), lambda i: (i,)),
                  smem(pstart.shape, lambda i: (0,)),
                  pl.BlockSpec(memory_space=pl.ANY),
                  pl.BlockSpec((tm, LANES), row), pl.BlockSpec((tm, d), row),
                  pl.BlockSpec((1, d), lambda i: (0, 0))],
        out_specs=pl.BlockSpec((tm, d), row),
        out_shape=jax.ShapeDtypeStruct((n, d), F32),
        scratch_shapes=[pltpu.VMEM((TOP_K, tm, d), F32), pltpu.SemaphoreType.DMA(())],
        compiler_params=_cparams(("arbitrary",)),
        name="moe_combine",
    )(e_flat, rk_flat, pstart, y_rows, gt, h1, gf)


def _tile(n, pref):
    t = min(pref, n)
    while n % t:
        t //= 2
    return t


def kernel(x_prompt, x_sample, cache_k, cache_v, state_conv, state_h, page_table, meta_tokens, norm1_g, w_in, lambda_q1, lambda_k1, lambda_q2, lambda_k2, subln_g, conv_w, conv_b, rg_wa, rg_ba, rg_wx, rg_bx, rg_lambda, w_attn_proj, w_lru_proj, w_out, norm2_g, router_w, router_b, w_gate, b_gate, w_up, b_up, w_down, b_down, final_norm_g):
    depth = w_in.shape[0]
    assert depth == 1, "single-layer stack"
    b, t, d = x_prompt.shape
    bd, s_new, _ = x_sample.shape
    n_p, n_s = b * t, bd * s_new
    row1 = lambda a: a.reshape(1, -1)

    w_in_bf = w_in[0].astype(BF16)
    g1 = row1(norm1_g[0])
    lv = jnp.stack([lambda_q1[0], lambda_k1[0], lambda_q2[0], lambda_k2[0]])
    sg = row1(subln_g[0])
    lru_p = dict(conv_w=conv_w[0], conv_b=row1(conv_b[0]), wa=rg_wa[0].astype(BF16), ba=row1(rg_ba[0]),
                 wx=rg_wx[0].astype(BF16), bx=row1(rg_bx[0]), lam=row1(rg_lambda[0]))
    rw = jnp.zeros((d, LANES), BF16).at[:, :N_EXPERTS].set(router_w[0].astype(BF16))
    rb = jnp.zeros((1, LANES), F32).at[0, :N_EXPERTS].set(router_b[0])
    post_p = dict(w_attn_proj=w_attn_proj[0].astype(BF16), w_lru_proj=w_lru_proj[0].astype(BF16),
                  w_out=w_out[0].astype(BF16), norm2_g=row1(norm2_g[0]), router_w=rw, router_b=rb)
    exp_p = dict(w_gate=w_gate[0], b_gate=b_gate[0][:, None, :], w_up=w_up[0], b_up=b_up[0][:, None, :],
                 w_down=w_down[0], b_down=b_down[0][:, None, :])

    xp = x_prompt.reshape(n_p, d)
    xs = x_sample.reshape(n_s, d)
    qbP, kbP, vbP, k3P, v3P, xrP, yrP, gaP, grP = _inproj(xp, g1, w_in_bf, _tile(n_p, ROW_TILE))
    qbS, kbS, vbS, k3S, v3S, xrS, yrS, gaS, grS = _inproj(xs, g1, w_in_bf, _tile(n_s, ROW_TILE))
    _, kbM, vbM, k3M, v3M, xrM, yrM, _, _ = _inproj(meta_tokens, g1, w_in_bf, N_META)

    oP = _attn_prompt(lv, qbP.reshape(b, t, d), kbP.reshape(b, t, d), vbP.reshape(b, t, d), kbM, vbM, sg)
    k_new = k3S.reshape(bd, s_new, d)
    v_new = v3S.reshape(bd, s_new, d)
    oS = _attn_sample(page_table, lv, qbS.astype(F32).reshape(bd, s_new, d), k_new, v_new, sg,
                      cache_k[0], cache_v[0])

    gb = LRU_GB
    zc = jnp.zeros((gb, CONV_W - 1, d), F32)
    zh = jnp.zeros((gb, d), F32)
    bc = lambda a: jnp.broadcast_to(a[None], (gb,) + a.shape)
    _, hM = _lru(bc(xrM), bc(yrM), zc, zh, lru_p, N_META, N_META)
    cbP = jnp.broadcast_to(xrM[None, N_META - (CONV_W - 1):], (b, CONV_W - 1, d))
    h0P = jnp.broadcast_to(hM[0:1], (b, d))
    xrP3 = xrP.reshape(b, t, d)
    hgP, hlP = _lru(xrP3, yrP.reshape(b, t, d), cbP, h0P, lru_p, _tile(t, LRU_TT), _tile(t, LRU_TT))
    t_pad = SUBLANES
    padt = lambda a: jnp.pad(a.reshape(bd, s_new, d), ((0, 0), (0, t_pad - s_new), (0, 0)))
    xrS3 = xrS.reshape(bd, s_new, d)
    hgS, hlS = _lru(padt(xrS), padt(yrS), state_conv[0], state_h[0], lru_p, t_pad, s_new)
    hgS = hgS[:, :s_new].reshape(n_s, d)

    cnt0 = jnp.zeros((1, LANES), F32)
    h1P, xnP, eiP, gtP, rkP, cntP = _post(oP.reshape(n_p, d), hgP.reshape(n_p, d), gaP, grP, xp, post_p,
                                          cnt0, _tile(n_p, ROW_TILE))
    h1S, xnS, eiS, gtS, rkS, cnt = _post(oS.reshape(n_s, d).astype(BF16), hgS, gaS, grS, xs, post_p,
                                         cntP, _tile(n_s, ROW_TILE))

    n_tok = n_p + n_s
    tmx = MOE_TILE
    counts = cnt[0, :N_EXPERTS].astype(I32)
    padded = (counts + tmx - 1) // tmx * tmx
    pend = jnp.cumsum(padded)
    pstart = (pend - padded).astype(I32)
    n_rows = -(-(n_tok * TOP_K + N_EXPERTS * (tmx - 1)) // tmx) * tmx
    n_blk = n_rows // tmx
    blk_e = jnp.minimum(jnp.searchsorted(pend, jnp.arange(n_blk, dtype=I32) * tmx, side="right"),
                        N_EXPERTS - 1).astype(I32)
    n_used = (pend[-1:] // tmx).astype(I32)
    flat = lambda a: a[:, :TOP_K].reshape(-1)

    rows = jnp.zeros((n_rows, d), F32)
    rows = _dispatch(flat(eiP), flat(rkP), pstart, xnP, rows, _tile(n_p, DISPATCH_TILE))
    rows = _dispatch(flat(eiS), flat(rkS), pstart, xnS, rows, _tile(n_s, DISPATCH_TILE))
    y_rows = _experts(blk_e, n_used, rows, exp_p, tmx)
    gf = row1(final_norm_g)
    yP = _combine(flat(eiP), flat(rkP), pstart, y_rows, gtP, h1P, gf, _tile(n_p, COMBINE_TILE))
    yS = _combine(flat(eiS), flat(rkS), pstart, y_rows, gtS, h1S, gf, _tile(n_s, COMBINE_TILE))

    def with_meta(x3, m3):
        m = jnp.broadcast_to(m3.reshape(1, N_META, N_HEADS, HEAD_W), (b, N_META, N_HEADS, HEAD_W))
        return jnp.concatenate([m, x3.reshape(b, t, N_HEADS, HEAD_W)], axis=1)[None]

    nc = CONV_W - 1
    return (yP.reshape(b, t, d), yS.reshape(bd, s_new, d),
            with_meta(k3P, k3M), with_meta(v3P, v3M),
            xrP3[:, t - nc:][None], hlP[None],
            k3S.reshape(1, bd, s_new, N_HEADS, HEAD_W), v3S.reshape(1, bd, s_new, N_HEADS, HEAD_W),
            xrS3[:, s_new - nc:][None], hlS[None])
```

```python
import functools
import math

import jax
import jax.numpy as jnp
from jax import lax
from jax.experimental import pallas as pl
from jax.experimental.pallas import tpu as pltpu

F32 = jnp.float32
BF16 = jnp.bfloat16
I32 = jnp.int32

LANES = 128
SUBLANES = 8
N_HEADS = 8
HEAD_W = 128
MAP_W = 64
N_META = 16
CONV_W = 4
LRU_BLOCKS = 4
LRU_C = 8.0
N_EXPERTS = 32
TOP_K = 4
SWIGLU_LIMIT = 7.0
SWIGLU_ALPHA = 1.702
EPS = 1e-6
SUBLN_EPS = 1e-5
LAM_INIT = 0.8 - 0.6 * math.exp(-0.3 * 0)
NEG_BIG = -1e30
VMEM_LIMIT = 56 * 1024 * 1024

ROW_TILE = 256
MOE_TILE = 256
ATT_TILE = 512
ATT_HEADS = 2
LRU_TT = 64
LRU_GB = 8
PAGES_PER_STEP = 8
DISPATCH_TILE = 512
COMBINE_TILE = 128
DMA_UNROLL = 4


def _cparams(sem):
    return pltpu.CompilerParams(dimension_semantics=sem, vmem_limit_bytes=VMEM_LIMIT)


def _tile(n, pref):
    t = min(pref, n)
    while n % t:
        t //= 2
    return t


def _sigmoid(x):
    return 1.0 / (1.0 + jnp.exp(-x))


def _expm1(x):
    u = jnp.exp(x)
    um1 = u - 1.0
    return jnp.where(u == 1.0, x, jnp.where(um1 == -1.0, -1.0, um1 * x / jnp.log(u)))


def _lam_from(lv):
    a = jnp.sum(lv[0:1, :] * lv[1:2, :], axis=-1, keepdims=True)
    b = jnp.sum(lv[2:3, :] * lv[3:4, :], axis=-1, keepdims=True)
    return jnp.exp(a) - jnp.exp(b) + LAM_INIT


def _subln(o, g):
    ms = jnp.mean(o * o, axis=-1, keepdims=True)
    return ((o * lax.rsqrt(ms + SUBLN_EPS)) * g) * (1.0 - LAM_INIT)


def _inproj_kernel(x_ref, g_ref, w_ref, qb_ref, kb_ref, vb_ref, k3_ref, v3_ref,
                   xr_ref, yr_ref, ga_ref, gr_ref):
    tm, d = x_ref.shape
    x = x_ref[...]
    ms = jnp.mean(x * x, axis=-1, keepdims=True)
    ub = ((x * lax.rsqrt(ms + EPS)) * g_ref[...]).astype(BF16)

    def proj(j):
        return jnp.dot(ub, w_ref[:, j * d:(j + 1) * d], preferred_element_type=F32)

    def store_heads(ref, val):
        for s in range(N_HEADS):
            ref[pl.ds(s, tm, stride=N_HEADS), :] = val[:, s * HEAD_W:(s + 1) * HEAD_W]

    q = proj(0)
    qb_ref[...] = (q * (MAP_W ** -0.5)).astype(BF16)
    k = proj(1)
    kb_ref[...] = k.astype(BF16)
    store_heads(k3_ref, k)
    v = proj(2)
    vb_ref[...] = v.astype(BF16)
    store_heads(v3_ref, v)
    xr_ref[...] = proj(3)
    yr_ref[...] = proj(4)
    ga_ref[...] = proj(5)
    gr_ref[...] = proj(6)


def _inproj(x, g, w_bf, tm, seq_prefix=None):
    r, d = x.shape
    assert r % tm == 0
    row = lambda i: (i, 0)
    f32o = jax.ShapeDtypeStruct((r, d), F32)
    bfo = jax.ShapeDtypeStruct((r, d), BF16)
    blk = pl.BlockSpec((tm, d), row)
    if seq_prefix is None:
        h3o = jax.ShapeDtypeStruct((r * N_HEADS, HEAD_W), F32)
        blk3 = pl.BlockSpec((tm * N_HEADS, HEAD_W), row)
    else:
        t, n_pre = seq_prefix
        assert t % tm == 0
        tiles = t // tm
        h3o = jax.ShapeDtypeStruct((r // t * (t + n_pre) * N_HEADS, HEAD_W), F32)
        blk3 = pl.BlockSpec(
            (pl.Element(tm * N_HEADS), pl.Element(HEAD_W)),
            lambda i: (((i // tiles) * (t + n_pre) + n_pre + (i % tiles) * tm) * N_HEADS, 0))
    return pl.pallas_call(
        _inproj_kernel,
        grid=(r // tm,),
        in_specs=[blk, pl.BlockSpec((1, d), lambda i: (0, 0)),
                  pl.BlockSpec(w_bf.shape, lambda i: (0, 0), pipeline_mode=pl.Buffered(1))],
        out_specs=[blk, blk, blk, blk3, blk3, blk, blk, blk, blk],
        out_shape=[bfo, bfo, bfo, h3o, h3o, f32o, f32o, f32o, f32o],
        compiler_params=_cparams(("parallel",)),
        name="inproj",
    )(x, g, w_bf)


def _fill_prefix_kernel(m_ref, buf_ref, o_ref):
    del buf_ref
    o_ref[...] = m_ref[...]


def _fill_prefix(m3, buf, n_seq):
    pre = m3.shape[0]
    stride = buf.shape[0] // n_seq
    assert stride % pre == 0
    return pl.pallas_call(
        _fill_prefix_kernel,
        grid=(n_seq,),
        in_specs=[pl.BlockSpec(m3.shape, lambda s: (0, 0)), pl.BlockSpec(memory_space=pl.ANY)],
        out_specs=pl.BlockSpec(m3.shape, lambda s: (s * (stride // pre), 0)),
        out_shape=jax.ShapeDtypeStruct(buf.shape, buf.dtype),
        input_output_aliases={1: 0},
        compiler_params=_cparams(("arbitrary",)),
        name="fill_prefix",
    )(m3, buf)


def _attn_prompt_kernel(lv_ref, q_ref, k_ref, v_ref, km_ref, vm_ref, g_ref, o_ref, *, tq, hp):
    qi = pl.program_id(2)
    lam = _lam_from(lv_ref[...])
    lane = lax.broadcasted_iota(I32, (tq, HEAD_W), 1)
    heads = [slice(h * HEAD_W, (h + 1) * HEAD_W) for h in range(hp)]

    def qq_of(sl):
        q = q_ref[0, :, sl]
        zero = jnp.zeros_like(q)
        return jnp.concatenate([jnp.where(lane < MAP_W, q, zero), jnp.where(lane >= MAP_W, q, zero)], axis=0)

    qqs = [qq_of(sl) for sl in heads]

    def scores(qq, kc):
        return lax.dot_general(qq, kc, (((1,), (1,)), ((), ())), preferred_element_type=F32)

    def update(state, s, vc):
        m, l, acc = state
        m_new = jnp.maximum(m, jnp.max(s, axis=-1, keepdims=True))
        alpha = jnp.exp(m - m_new)
        p = jnp.exp(s - m_new)
        l = alpha * l + jnp.sum(p, axis=-1, keepdims=True)
        acc = alpha * acc + jnp.dot(p.astype(BF16), vc, preferred_element_type=F32)
        return m_new, l, acc

    states = []
    for qq, sl in zip(qqs, heads):
        s0 = scores(qq, km_ref[:, sl])
        m = jnp.max(s0, axis=-1, keepdims=True)
        p0 = jnp.exp(s0 - m)
        states.append((m, jnp.sum(p0, axis=-1, keepdims=True),
                       jnp.dot(p0.astype(BF16), vm_ref[:, sl], preferred_element_type=F32)))

    def body(j, states):
        off = pl.multiple_of(j * tq, tq)
        return tuple(update(st, scores(qq, k_ref[0, pl.ds(off, tq), sl]), v_ref[0, pl.ds(off, tq), sl])
                     for st, qq, sl in zip(states, qqs, heads))

    states = lax.fori_loop(0, qi, body, tuple(states))

    off = pl.multiple_of(qi * tq, tq)
    row = lax.broadcasted_iota(I32, (2 * tq, tq), 0)
    col = lax.broadcasted_iota(I32, (2 * tq, tq), 1)
    causal = col <= jnp.where(row >= tq, row - tq, row)
    for st, qq, sl in zip(states, qqs, heads):
        s = jnp.where(causal, scores(qq, k_ref[0, pl.ds(off, tq), sl]), -jnp.inf)
        m, l, acc = update(st, s, v_ref[0, pl.ds(off, tq), sl])
        o = acc / l
        o = o[:tq] - lam * o[tq:]
        o_ref[0, :, sl] = _subln(o, g_ref[...]).astype(o_ref.dtype)


def _attn_prompt(lv, qb, kb, vb, kmb, vmb, subln_g):
    b, t, d = qb.shape
    tq = _tile(t, ATT_TILE)
    hp = ATT_HEADS
    w = hp * HEAD_W
    kern = functools.partial(_attn_prompt_kernel, tq=tq, hp=hp)
    return pl.pallas_call(
        kern,
        grid=(b, N_HEADS // hp, t // tq),
        in_specs=[pl.BlockSpec(lv.shape, lambda bi, h, i: (0, 0)),
                  pl.BlockSpec((1, tq, w), lambda bi, h, i: (bi, i, h)),
                  pl.BlockSpec((1, t, w), lambda bi, h, i: (bi, 0, h)),
                  pl.BlockSpec((1, t, w), lambda bi, h, i: (bi, 0, h)),
                  pl.BlockSpec((N_META, w), lambda bi, h, i: (0, h)),
                  pl.BlockSpec((N_META, w), lambda bi, h, i: (0, h)),
                  pl.BlockSpec((1, HEAD_W), lambda bi, h, i: (0, 0))],
        out_specs=pl.BlockSpec((1, tq, w), lambda bi, h, i: (bi, i, h)),
        out_shape=jax.ShapeDtypeStruct((b, t, d), BF16),
        compiler_params=_cparams(("parallel", "parallel", "arbitrary")),
        name="attn_prompt",
    )(lv, qb, kb, vb, kmb, vmb, subln_g)


def _attn_sample_kernel(pt_ref, lv_ref, q_ref, kn_ref, vn_ref, g_ref, *rest, pp, s_new):
    kpages = rest[:pp]
    vpages = rest[pp:2 * pp]
    o_ref = rest[2 * pp]
    qq_scr, bias_scr, m_scr, l_scr, acc_scr = rest[2 * pp + 1:]
    g = pl.program_id(1)
    ng = pl.num_programs(1)
    hr = 2 * s_new
    rows = N_HEADS * hr

    def head_bias(shape):
        row = lax.broadcasted_iota(I32, shape, 0)
        col = lax.broadcasted_iota(I32, shape, 1)
        return row, col, (col % N_HEADS) == (row // hr)

    @pl.when(g == 0)
    def _():
        lane = lax.broadcasted_iota(I32, (s_new, HEAD_W), 1)
        parts = []
        for h in range(N_HEADS):
            qh = q_ref[0, :, h * HEAD_W:(h + 1) * HEAD_W]
            parts += [jnp.where(lane < MAP_W, qh, 0.0), jnp.where(lane >= MAP_W, qh, 0.0)]
        qq_scr[...] = jnp.concatenate(parts, axis=0).astype(BF16)
        _, _, same = head_bias(bias_scr.shape)
        bias_scr[...] = jnp.where(same, 0.0, -jnp.inf)
        m_scr[...] = jnp.full(m_scr.shape, -jnp.inf, F32)
        l_scr[...] = jnp.zeros(l_scr.shape, F32)
        acc_scr[...] = jnp.zeros(acc_scr.shape, F32)

    def scores(kc):
        return lax.dot_general(qq_scr[...], kc, (((1,), (1,)), ((), ())), preferred_element_type=F32)

    def update(s_list, v_list):
        m_prev = m_scr[...]
        s_max = s_list[0]
        for s in s_list[1:]:
            s_max = jnp.maximum(s_max, s)
        m_new = jnp.maximum(m_prev, jnp.max(s_max, axis=-1, keepdims=True))
        alpha = jnp.exp(m_prev - m_new)
        l_add = None
        pv = None
        for s, vc in zip(s_list, v_list):
            p = jnp.exp(s - m_new[:, :1])
            ps = jnp.sum(p, axis=-1, keepdims=True)
            l_add = ps if l_add is None else l_add + ps
            d = jnp.dot(p.astype(BF16), vc, preferred_element_type=F32)
            pv = d if pv is None else pv + d
        m_scr[...] = m_new
        l_scr[...] = alpha * l_scr[...] + l_add
        acc_scr[...] = alpha * acc_scr[...] + pv

    update([scores(kp[...].astype(BF16)) + bias_scr[...] for kp in kpages],
           [vp[...].astype(BF16) for vp in vpages])

    @pl.when(g == ng - 1)
    def _():
        lam = _lam_from(lv_ref[...])
        row, col, same = head_bias((rows, s_new * N_HEADS))
        causal = (col // N_HEADS) <= ((row % hr) % s_new)
        s = jnp.where(same & causal, scores(kn_ref[0].astype(BF16)), -jnp.inf)
        update([s], [vn_ref[0].astype(BF16)])
        o = acc_scr[...] / l_scr[...]
        for h in range(N_HEADS):
            oh = o[h * hr:h * hr + s_new] - lam * o[h * hr + s_new:(h + 1) * hr]
            o_ref[0, :, h * HEAD_W:(h + 1) * HEAD_W] = _subln(oh, g_ref[...])


def _attn_sample(page_table, lv, q, kn, vn, subln_g, cache_k, cache_v):
    bd, s_new, d = q.shape
    n_pages = page_table.shape[1]
    page_rows = cache_k.shape[1]
    pp = PAGES_PER_STEP if n_pages % PAGES_PER_STEP == 0 else 1
    rows = N_HEADS * 2 * s_new
    pt = page_table.reshape(-1)
    seq = lambda b, g, pt: (b, 0, 0)

    def page_spec(p):
        return pl.BlockSpec((None, page_rows, HEAD_W),
                            lambda b, g, pt, p=p: (pt[b * n_pages + g * pp + p], 0, 0))

    grid_spec = pltpu.PrefetchScalarGridSpec(
        num_scalar_prefetch=1,
        grid=(bd, n_pages // pp),
        in_specs=[pl.BlockSpec(lv.shape, lambda b, g, pt: (0, 0)),
                  pl.BlockSpec((1, s_new, d), seq),
                  pl.BlockSpec((1, s_new * N_HEADS, HEAD_W), seq),
                  pl.BlockSpec((1, s_new * N_HEADS, HEAD_W), seq),
                  pl.BlockSpec((1, HEAD_W), lambda b, g, pt: (0, 0))]
                 + [page_spec(p) for p in range(pp)] + [page_spec(p) for p in range(pp)],
        out_specs=pl.BlockSpec((1, s_new, d), seq),
        scratch_shapes=[pltpu.VMEM((rows, HEAD_W), BF16),
                        pltpu.VMEM((rows, page_rows), F32),
                        pltpu.VMEM((rows, LANES), F32),
                        pltpu.VMEM((rows, LANES), F32),
                        pltpu.VMEM((rows, HEAD_W), F32)],
    )
    kern = functools.partial(_attn_sample_kernel, pp=pp, s_new=s_new)
    return pl.pallas_call(
        kern,
        grid_spec=grid_spec,
        out_shape=jax.ShapeDtypeStruct((bd, s_new, d), F32),
        compiler_params=_cparams(("parallel", "arbitrary")),
        name="attn_sample",
    )(pt, lv, q, kn, vn, subln_g, *([cache_k] * pp), *([cache_v] * pp))


def _lru_kernel(xr_ref, yr_ref, cb0_ref, h0_ref, cw_ref, cb_ref, wa_ref, ba_ref, wx_ref, bx_ref,
                lam_ref, hg_ref, hl_ref, xbuf, a_scr, u_scr, h_scr, *, n_steps):
    gb, tt, w = xr_ref.shape
    bw = w // LRU_BLOCKS
    i = pl.program_id(1)
    halo = SUBLANES

    @pl.when(i == 0)
    def _():
        xbuf[:, halo - (CONV_W - 1):halo, :] = cb0_ref[...]
        h_scr[...] = h0_ref[...]

    xbuf[:, halo:halo + tt, :] = xr_ref[...]
    xc = cb_ref[...] + xbuf[:, halo - 3:halo - 3 + tt, :] * cw_ref[0:1, :]
    for j in range(1, CONV_W):
        xc = xc + xbuf[:, halo - 3 + j:halo - 3 + j + tt, :] * cw_ref[j:j + 1, :]
    xbuf[:, halo - (CONV_W - 1):halo, :] = xbuf[:, halo + tt - (CONV_W - 1):halo + tt, :]

    xc2 = xc.reshape(gb * tt, w)
    xcb = xc2.astype(BF16)
    r_parts, i_parts = [], []
    for c in range(LRU_BLOCKS):
        xs = xcb[:, c * bw:(c + 1) * bw]
        r_parts.append(jnp.dot(xs, wa_ref[c], preferred_element_type=F32))
        i_parts.append(jnp.dot(xs, wx_ref[c], preferred_element_type=F32))
    r = _sigmoid(jnp.concatenate(r_parts, axis=-1) + ba_ref[...])
    ig = _sigmoid(jnp.concatenate(i_parts, axis=-1) + bx_ref[...])
    nl = -lam_ref[...]
    softplus = jnp.maximum(nl, 0.0) + jnp.log1p(jnp.exp(-jnp.abs(nl)))
    log_a = (-LRU_C * r) * softplus
    a_scr[...] = jnp.exp(log_a).reshape(gb, tt, w)
    mult = jnp.sqrt(-_expm1(2.0 * log_a))
    u_scr[...] = (xc2 * ig * mult).reshape(gb, tt, w)

    def step(t, h):
        h = a_scr[:, t, :] * h + u_scr[:, t, :]
        u_scr[:, t, :] = h
        return h

    h = lax.fori_loop(0, n_steps, step, h_scr[...])
    h_scr[...] = h
    hl_ref[...] = h
    y = yr_ref[...]
    gelu = y * (0.5 * (1.0 + jnp.tanh(math.sqrt(2.0 / math.pi) * (y + 0.044715 * (y * y * y)))))
    hg_ref[...] = (u_scr[...] * gelu).astype(hg_ref.dtype)


def _lru(xr, yr, cb0, h0, p, tt, n_steps):
    b, t, w = xr.shape
    gb = LRU_GB
    assert b % gb == 0 and t % tt == 0 and (n_steps == tt or t == tt)
    bw = w // LRU_BLOCKS
    blk = pl.BlockSpec((gb, tt, w), lambda g, i: (g, i, 0))
    vec = pl.BlockSpec((1, w), lambda g, i: (0, 0))
    wsp = pl.BlockSpec((LRU_BLOCKS, bw, bw), lambda g, i: (0, 0, 0))
    kern = functools.partial(_lru_kernel, n_steps=n_steps)
    return pl.pallas_call(
        kern,
        grid=(b // gb, t // tt),
        in_specs=[blk, blk,
                  pl.BlockSpec((gb, CONV_W - 1, w), lambda g, i: (g, 0, 0)),
                  pl.BlockSpec((gb, w), lambda g, i: (g, 0)),
                  pl.BlockSpec((CONV_W, w), lambda g, i: (0, 0)), vec, wsp, vec, wsp, vec, vec],
        out_specs=[blk, pl.BlockSpec((gb, w), lambda g, i: (g, 0))],
        out_shape=[jax.ShapeDtypeStruct((b, t, w), BF16), jax.ShapeDtypeStruct((b, w), F32)],
        scratch_shapes=[pltpu.VMEM((gb, tt + SUBLANES, w), F32), pltpu.VMEM((gb, tt, w), F32),
                        pltpu.VMEM((gb, tt, w), F32), pltpu.VMEM((gb, w), F32)],
        compiler_params=_cparams(("parallel", "arbitrary")),
        name="lru",
    )(xr, yr, cb0, h0, p["conv_w"], p["conv_b"], p["wa"], p["ba"], p["wx"], p["bx"], p["lam"])


def _post_kernel(o_ref, hg_ref, ga_ref, gr_ref, x_ref, wap_ref, wlp_ref, wo_ref, g2_ref, rw_ref,
                 rb_ref, cin_ref, h1_ref, xn_ref, ei_ref, gt_ref, rk_ref, cnt_ref, carry):
    tm = x_ref.shape[0]

    @pl.when(pl.program_id(0) == 0)
    def _():
        carry[...] = cin_ref[...]

    attn_out = jnp.dot(o_ref[...], wap_ref[...], preferred_element_type=F32)
    lru_out = jnp.dot(hg_ref[...], wlp_ref[...], preferred_element_type=F32)
    merged = _sigmoid(ga_ref[...]) * attn_out + _sigmoid(gr_ref[...]) * lru_out
    h1 = x_ref[...] + jnp.dot(merged.astype(BF16), wo_ref[...], preferred_element_type=F32)
    h1_ref[...] = h1
    ms = jnp.mean(h1 * h1, axis=-1, keepdims=True)
    xn = (h1 * lax.rsqrt(ms + EPS)) * g2_ref[...]
    xn_ref[...] = xn

    lane = lax.broadcasted_iota(I32, (tm, LANES), 1)
    logits = jnp.dot(xn.astype(BF16), rw_ref[...], preferred_element_type=F32) + rb_ref[...]
    lg = jnp.where(lane < N_EXPERTS, logits, NEG_BIG)
    tops, idxs, hots = [], [], []
    for _ in range(TOP_K):
        mx = jnp.max(lg, axis=-1, keepdims=True)
        idx = jnp.min(jnp.where(lg == mx, lane, LANES), axis=-1, keepdims=True)
        hot = lane == idx
        tops.append(mx); idxs.append(idx); hots.append(hot)
        lg = jnp.where(hot, NEG_BIG, lg)
    exps = [jnp.exp(tv - tops[0]) for tv in tops]
    den = exps[0] + exps[1] + exps[2] + exps[3]

    sel = jnp.zeros((tm, LANES), F32)
    for hot in hots:
        sel = sel + jnp.where(hot, 1.0, 0.0)
    r_i = lax.broadcasted_iota(I32, (tm, tm), 0)
    c_i = lax.broadcasted_iota(I32, (tm, tm), 1)
    ltri = jnp.where(c_i < r_i, 1.0, 0.0).astype(BF16)
    base = jnp.dot(ltri, sel.astype(BF16), preferred_element_type=F32) + carry[...]
    new_carry = carry[...] + jnp.sum(sel, axis=0, keepdims=True)
    carry[...] = new_carry
    cnt_ref[...] = new_carry

    ei = jnp.zeros((tm, LANES), I32)
    gt = jnp.zeros((tm, LANES), F32)
    rk = jnp.zeros((tm, LANES), I32)
    for k in range(TOP_K):
        rank = jnp.sum(jnp.where(hots[k], base, 0.0), axis=-1, keepdims=True)
        ei = jnp.where(lane == k, idxs[k], ei)
        gt = jnp.where(lane == k, exps[k] / den, gt)
        rk = jnp.where(lane == k, rank.astype(I32), rk)
    ei_ref[...] = ei
    gt_ref[...] = gt
    rk_ref[...] = rk


def _post(o, hg, ga, gr, x, p, cnt_in, tm):
    r, d = x.shape
    assert r % tm == 0
    row = lambda i: (i, 0)
    fix = lambda i: (0, 0)
    blk = pl.BlockSpec((tm, d), row)
    wsp = pl.BlockSpec((d, d), fix, pipeline_mode=pl.Buffered(1))
    lsp = pl.BlockSpec((tm, LANES), row)
    vec = pl.BlockSpec((1, LANES), fix)
    return pl.pallas_call(
        _post_kernel,
        grid=(r // tm,),
        in_specs=[blk, blk, blk, blk, blk, wsp, wsp, wsp, pl.BlockSpec((1, d), fix),
                  pl.BlockSpec((d, LANES), fix), vec, vec],
        out_specs=[blk, blk, lsp, lsp, lsp, vec],
        out_shape=[jax.ShapeDtypeStruct((r, d), F32), jax.ShapeDtypeStruct((r, d), F32),
                   jax.ShapeDtypeStruct((r, LANES), I32), jax.ShapeDtypeStruct((r, LANES), F32),
                   jax.ShapeDtypeStruct((r, LANES), I32), jax.ShapeDtypeStruct((1, LANES), F32)],
        scratch_shapes=[pltpu.VMEM((1, LANES), F32)],
        compiler_params=_cparams(("arbitrary",)),
        name="post_router",
    )(o, hg, ga, gr, x, p["w_attn_proj"], p["w_lru_proj"], p["w_out"], p["norm2_g"],
      p["router_w"], p["router_b"], cnt_in)


def _dispatch_kernel(e_ref, rk_ref, ps_ref, xn_ref, rows_in_ref, rows_ref, sem, *, tm):
    del rows_in_ref

    def row_copy(j, d):
        return pltpu.make_async_copy(xn_ref.at[pl.ds(j, 1), :], rows_ref.at[pl.ds(d, 1), :], sem)

    def issue(jj, c):
        for u in range(DMA_UNROLL):
            j = jj * DMA_UNROLL + u
            for k in range(TOP_K):
                n = j * TOP_K + k
                row_copy(j, ps_ref[e_ref[n]] + rk_ref[n]).start()
        return c

    lax.fori_loop(0, tm // DMA_UNROLL, issue, 0)
    for _ in range(TOP_K):
        pltpu.make_async_copy(xn_ref, rows_ref.at[pl.ds(0, tm), :], sem).wait()


def _dispatch(e_flat, rk_flat, pstart, xn, rows, tm):
    n, d = xn.shape
    assert n % tm == 0
    smem = lambda shape, imap: pl.BlockSpec(shape, imap, memory_space=pltpu.SMEM)
    kern = functools.partial(_dispatch_kernel, tm=tm)
    return pl.pallas_call(
        kern,
        grid=(n // tm,),
        in_specs=[smem((tm * TOP_K,), lambda i: (i,)), smem((tm * TOP_K,), lambda i: (i,)),
                  smem(pstart.shape, lambda i: (0,)),
                  pl.BlockSpec((tm, d), lambda i: (i, 0)), pl.BlockSpec(memory_space=pl.ANY)],
        out_specs=pl.BlockSpec(memory_space=pl.ANY),
        out_shape=jax.ShapeDtypeStruct(rows.shape, rows.dtype),
        scratch_shapes=[pltpu.SemaphoreType.DMA(())],
        input_output_aliases={4: 0},
        compiler_params=_cparams(("arbitrary",)),
        name="moe_dispatch",
    )(e_flat, rk_flat, pstart, xn, rows)


def _expert_kernel(be_ref, nu_ref, x_ref, wg_ref, bg_ref, wu_ref, bu_ref, wd_ref, bd_ref, y_ref,
                   wg_s, wu_s, wd_s):
    j = pl.program_id(0)
    prev = be_ref[jnp.maximum(j - 1, 0)]

    @pl.when((j == 0) | (be_ref[j] != prev))
    def _():
        wg_s[...] = wg_ref[0].astype(BF16)
        wu_s[...] = wu_ref[0].astype(BF16)
        wd_s[...] = wd_ref[0].astype(BF16)

    @pl.when(j < nu_ref[0])
    def _():
        xb = x_ref[...].astype(BF16)
        g = jnp.dot(xb, wg_s[...], preferred_element_type=F32) + bg_ref[0]
        up = jnp.dot(xb, wu_s[...], preferred_element_type=F32) + bu_ref[0]
        g = jnp.minimum(g, SWIGLU_LIMIT)
        up = jnp.clip(up, -SWIGLU_LIMIT, SWIGLU_LIMIT)
        h = (up + 1.0) * (g * _sigmoid(SWIGLU_ALPHA * g))
        y_ref[...] = jnp.dot(h.astype(BF16), wd_s[...], preferred_element_type=F32) + bd_ref[0]


def _experts(blk_e, n_used, rows, p, tm):
    n_rows, d = rows.shape
    dff = p["w_gate"].shape[2]
    n_blk = n_rows // tm
    wsp = lambda shape: pl.BlockSpec((1,) + shape, lambda j, be, nu: (be[j], 0, 0))
    grid_spec = pltpu.PrefetchScalarGridSpec(
        num_scalar_prefetch=2,
        grid=(n_blk,),
        in_specs=[pl.BlockSpec((tm, d), lambda j, be, nu: (j, 0)),
                  wsp((d, dff)), wsp((1, dff)), wsp((d, dff)), wsp((1, dff)),
                  wsp((dff, d)), wsp((1, d))],
        out_specs=pl.BlockSpec((tm, d), lambda j, be, nu: (j, 0)),
        scratch_shapes=[pltpu.VMEM((d, dff), BF16), pltpu.VMEM((d, dff), BF16),
                        pltpu.VMEM((dff, d), BF16)],
    )
    return pl.pallas_call(
        _expert_kernel,
        grid_spec=grid_spec,
        out_shape=jax.ShapeDtypeStruct((n_rows, d), F32),
        compiler_params=_cparams(("arbitrary",)),
        name="moe_experts",
    )(blk_e, n_used, rows, p["w_gate"], p["b_gate"], p["w_up"], p["b_up"], p["w_down"], p["b_down"])


def _combine_kernel(e_ref, rk_ref, ps_ref, y_ref, gt_ref, h1_ref, gf_ref, out_ref, buf, sem, *, tm):
    def row_copy(d, k, j):
        return pltpu.make_async_copy(y_ref.at[pl.ds(d, 1), :], buf.at[k, pl.ds(j, 1), :], sem)

    def issue(jj, c):
        for u in range(DMA_UNROLL):
            j = jj * DMA_UNROLL + u
            for k in range(TOP_K):
                n = j * TOP_K + k
                row_copy(ps_ref[e_ref[n]] + rk_ref[n], k, j).start()
        return c

    lax.fori_loop(0, tm // DMA_UNROLL, issue, 0)
    for k in range(TOP_K):
        pltpu.make_async_copy(y_ref.at[pl.ds(0, tm), :], buf.at[k], sem).wait()

    gt = gt_ref[...]
    y = buf[0] * gt[:, 0:1]
    for k in range(1, TOP_K):
        y = y + buf[k] * gt[:, k:k + 1]
    h2 = h1_ref[...] + y
    ms = jnp.mean(h2 * h2, axis=-1, keepdims=True)
    out_ref[...] = (h2 * lax.rsqrt(ms + EPS)) * gf_ref[...]


def _combine(e_flat, rk_flat, pstart, y_rows, gt, h1, gf, tm):
    n, d = h1.shape
    assert n % tm == 0
    smem = lambda shape, imap: pl.BlockSpec(shape, imap, memory_space=pltpu.SMEM)
    row = lambda i: (i, 0)
    kern = functools.partial(_combine_kernel, tm=tm)
    return pl.pallas_call(
        kern,
        grid=(n // tm,),
        in_specs=[smem((tm * TOP_K,), lambda i: (i,)), smem((tm * TOP_K,), lambda i: (i,)),
                  smem(pstart.shape, lambda i: (0,)),
                  pl.BlockSpec(memory_space=pl.ANY),
                  pl.BlockSpec((tm, LANES), row), pl.BlockSpec((tm, d), row),
                  pl.BlockSpec((1, d), lambda i: (0, 0))],
        out_specs=pl.BlockSpec((tm, d), row),
        out_shape=jax.ShapeDtypeStruct((n, d), F32),
        scratch_shapes=[pltpu.VMEM((TOP_K, tm, d), F32), pltpu.SemaphoreType.DMA(())],
        compiler_params=_cparams(("arbitrary",)),
        name="moe_combine",
    )(e_flat, rk_flat, pstart, y_rows, gt, h1, gf)


def kernel(x_prompt, x_sample, cache_k, cache_v, state_conv, state_h, page_table, meta_tokens, norm1_g, w_in, lambda_q1, lambda_k1, lambda_q2, lambda_k2, subln_g, conv_w, conv_b, rg_wa, rg_ba, rg_wx, rg_bx, rg_lambda, w_attn_proj, w_lru_proj, w_out, norm2_g, router_w, router_b, w_gate, b_gate, w_up, b_up, w_down, b_down, final_norm_g):
    depth = w_in.shape[0]
    assert depth == 1, "single-layer stack"
    b, t, d = x_prompt.shape
    bd, s_new, _ = x_sample.shape
    n_p, n_s = b * t, bd * s_new
    row1 = lambda a: a.reshape(1, -1)

    w_in_bf = w_in[0].astype(BF16)
    g1 = row1(norm1_g[0])
    lv = jnp.stack([lambda_q1[0], lambda_k1[0], lambda_q2[0], lambda_k2[0]])
    sg = row1(subln_g[0])
    lru_p = dict(conv_w=conv_w[0], conv_b=row1(conv_b[0]), wa=rg_wa[0].astype(BF16), ba=row1(rg_ba[0]),
                 wx=rg_wx[0].astype(BF16), bx=row1(rg_bx[0]), lam=row1(rg_lambda[0]))
    rw = jnp.zeros((d, LANES), BF16).at[:, :N_EXPERTS].set(router_w[0].astype(BF16))
    rb = jnp.zeros((1, LANES), F32).at[0, :N_EXPERTS].set(router_b[0])
    post_p = dict(w_attn_proj=w_attn_proj[0].astype(BF16), w_lru_proj=w_lru_proj[0].astype(BF16),
                  w_out=w_out[0].astype(BF16), norm2_g=row1(norm2_g[0]), router_w=rw, router_b=rb)
    exp_p = dict(w_gate=w_gate[0], b_gate=b_gate[0][:, None, :], w_up=w_up[0], b_up=b_up[0][:, None, :],
                 w_down=w_down[0], b_down=b_down[0][:, None, :])

    xp = x_prompt.reshape(n_p, d)
    xs = x_sample.reshape(n_s, d)
    qbP, kbP, vbP, k3P, v3P, xrP, yrP, gaP, grP = _inproj(xp, g1, w_in_bf, _tile(t, ROW_TILE), (t, N_META))
    qbS, kbS, vbS, k3S, v3S, xrS, yrS, gaS, grS = _inproj(xs, g1, w_in_bf, _tile(n_s, ROW_TILE))
    _, kbM, vbM, k3M, v3M, xrM, yrM, _, _ = _inproj(meta_tokens, g1, w_in_bf, N_META)

    oP = _attn_prompt(lv, qbP.reshape(b, t, d), kbP.reshape(b, t, d), vbP.reshape(b, t, d), kbM, vbM, sg)
    n_pool, page = cache_k.shape[1], cache_k.shape[2]
    oS = _attn_sample(page_table, lv, qbS.astype(F32).reshape(bd, s_new, d),
                      k3S.reshape(bd, s_new * N_HEADS, HEAD_W), v3S.reshape(bd, s_new * N_HEADS, HEAD_W), sg,
                      cache_k[0].reshape(n_pool, page * N_HEADS, HEAD_W),
                      cache_v[0].reshape(n_pool, page * N_HEADS, HEAD_W))

    gb = LRU_GB
    zc = jnp.zeros((gb, CONV_W - 1, d), F32)
    zh = jnp.zeros((gb, d), F32)
    bc = lambda a: jnp.broadcast_to(a[None], (gb,) + a.shape)
    _, hM = _lru(bc(xrM), bc(yrM), zc, zh, lru_p, N_META, N_META)
    cbP = jnp.broadcast_to(xrM[None, N_META - (CONV_W - 1):], (b, CONV_W - 1, d))
    h0P = jnp.broadcast_to(hM[0:1], (b, d))
    xrP3 = xrP.reshape(b, t, d)
    hgP, hlP = _lru(xrP3, yrP.reshape(b, t, d), cbP, h0P, lru_p, _tile(t, LRU_TT), _tile(t, LRU_TT))
    t_pad = SUBLANES
    padt = lambda a: jnp.pad(a.reshape(bd, s_new, d), ((0, 0), (0, t_pad - s_new), (0, 0)))
    xrS3 = xrS.reshape(bd, s_new, d)
    hgS, hlS = _lru(padt(xrS), padt(yrS), state_conv[0], state_h[0], lru_p, t_pad, s_new)
    hgS = hgS[:, :s_new].reshape(n_s, d)

    cnt0 = jnp.zeros((1, LANES), F32)
    h1P, xnP, eiP, gtP, rkP, cntP = _post(oP.reshape(n_p, d), hgP.reshape(n_p, d), gaP, grP, xp, post_p,
                                          cnt0, _tile(n_p, ROW_TILE))
    h1S, xnS, eiS, gtS, rkS, cnt = _post(oS.reshape(n_s, d).astype(BF16), hgS, gaS, grS, xs, post_p,
                                         cntP, _tile(n_s, ROW_TILE))

    n_tok = n_p + n_s
    tmx = MOE_TILE
    counts = cnt[0, :N_EXPERTS].astype(I32)
    padded = (counts + tmx - 1) // tmx * tmx
    pend = jnp.cumsum(padded)
    pstart = (pend - padded).astype(I32)
    n_rows = -(-(n_tok * TOP_K + N_EXPERTS * (tmx - 1)) // tmx) * tmx
    n_blk = n_rows // tmx
    blk_start = jnp.arange(n_blk, dtype=I32) * tmx
    blk_e = jnp.minimum(jnp.sum((pend[None, :] <= blk_start[:, None]).astype(I32), axis=1), N_EXPERTS - 1)
    n_used = (pend[-1:] // tmx).astype(I32)
    flat = lambda a: a[:, :TOP_K].reshape(-1)

    rows = jnp.zeros((n_rows, d), F32)
    rows = _dispatch(flat(eiP), flat(rkP), pstart, xnP, rows, _tile(n_p, DISPATCH_TILE))
    rows = _dispatch(flat(eiS), flat(rkS), pstart, xnS, rows, _tile(n_s, DISPATCH_TILE))
    y_rows = _experts(blk_e, n_used, rows, exp_p, tmx)
    gf = row1(final_norm_g)
    yP = _combine(flat(eiP), flat(rkP), pstart, y_rows, gtP, h1P, gf, _tile(n_p, COMBINE_TILE))
    yS = _combine(flat(eiS), flat(rkS), pstart, y_rows, gtS, h1S, gf, _tile(n_s, COMBINE_TILE))

    def with_meta(x3, m3):
        return _fill_prefix(m3, x3, b).reshape(1, b, t + N_META, N_HEADS, HEAD_W)

    nc = CONV_W - 1
    return (yP.reshape(b, t, d), yS.reshape(bd, s_new, d),
            with_meta(k3P, k3M), with_meta(v3P, v3M),
            xrP3[:, t - nc:][None], hlP[None],
            k3S.reshape(1, bd, s_new, N_HEADS, HEAD_W), v3S.reshape(1, bd, s_new, N_HEADS, HEAD_W),
            xrS3[:, s_new - nc:][None], hlS[None])
```

```python
import functools
import math

import jax
import jax.numpy as jnp
from jax import lax
from jax.experimental import pallas as pl
from jax.experimental.pallas import tpu as pltpu

F32 = jnp.float32
BF16 = jnp.bfloat16
I32 = jnp.int32

LANES = 128
SUBLANES = 8
N_HEADS = 8
HEAD_W = 128
MAP_W = 64
N_META = 16
CONV_W = 4
LRU_BLOCKS = 4
LRU_C = 8.0
N_EXPERTS = 32
TOP_K = 4
SWIGLU_LIMIT = 7.0
SWIGLU_ALPHA = 1.702
EPS = 1e-6
SUBLN_EPS = 1e-5
LAM_INIT = 0.8 - 0.6 * math.exp(-0.3 * 0)
NEG_BIG = -1e30
VMEM_LIMIT = 56 * 1024 * 1024

ROW_TILE = 256
MOE_TILE = 512
ATT_TILE = 512
ATT_HEADS = 2
ATT_STRIP = 32
LRU_TT = 64
LRU_GB = 8
PAGES_PER_STEP = 16
SOFTMAX_SETS = 4
DISPATCH_TILE = 512
COMBINE_TILE = 128
DMA_UNROLL = 4


def _cparams(sem):
    return pltpu.CompilerParams(dimension_semantics=sem, vmem_limit_bytes=VMEM_LIMIT)


def _tile(n, pref):
    t = min(pref, n)
    while n % t:
        t //= 2
    return t


def _store_row_tiles(ref, val):
    n = val.shape[0]
    for s in range(SUBLANES):
        ref[pl.ds(s, n, stride=SUBLANES), :] = val[:, s * LANES:(s + 1) * LANES]


def _load_row_tiles(ref, n):
    return jnp.concatenate([ref[pl.ds(s, n, stride=SUBLANES), :] for s in range(SUBLANES)], axis=-1)


def _sigmoid(x):
    return 1.0 / (1.0 + jnp.exp(-x))


def _expm1(x):
    u = jnp.exp(x)
    um1 = u - 1.0
    return jnp.where(u == 1.0, x, jnp.where(um1 == -1.0, -1.0, um1 * x / jnp.log(u)))


def _lam_from(lv):
    a = jnp.sum(lv[0:1, :] * lv[1:2, :], axis=-1, keepdims=True)
    b = jnp.sum(lv[2:3, :] * lv[3:4, :], axis=-1, keepdims=True)
    return jnp.exp(a) - jnp.exp(b) + LAM_INIT


def _subln(o, g):
    ms = jnp.mean(o * o, axis=-1, keepdims=True)
    return ((o * lax.rsqrt(ms + SUBLN_EPS)) * g) * (1.0 - LAM_INIT)


def _inproj_kernel(x_ref, g_ref, w_ref, qb_ref, kb_ref, vb_ref, k3_ref, v3_ref,
                   xr_ref, yr_ref, ga_ref, gr_ref):
    tm, d = x_ref.shape
    x = x_ref[...]
    ms = jnp.mean(x * x, axis=-1, keepdims=True)
    ub = ((x * lax.rsqrt(ms + EPS)) * g_ref[...]).astype(BF16)

    def proj(j):
        return jnp.dot(ub, w_ref[:, j * d:(j + 1) * d], preferred_element_type=F32)

    q = proj(0)
    qb_ref[...] = (q * (MAP_W ** -0.5)).astype(BF16)
    k = proj(1)
    kb_ref[...] = k.astype(BF16)
    _store_row_tiles(k3_ref, k)
    v = proj(2)
    vb_ref[...] = v.astype(BF16)
    _store_row_tiles(v3_ref, v)
    xr_ref[...] = proj(3)
    yr_ref[...] = proj(4)
    ga_ref[...] = proj(5)
    gr_ref[...] = proj(6)


def _inproj(x, g, w_bf, tm, seq_prefix=None):
    r, d = x.shape
    assert r % tm == 0
    row = lambda i: (i, 0)
    f32o = jax.ShapeDtypeStruct((r, d), F32)
    bfo = jax.ShapeDtypeStruct((r, d), BF16)
    blk = pl.BlockSpec((tm, d), row)
    if seq_prefix is None:
        h3o = jax.ShapeDtypeStruct((r * N_HEADS, HEAD_W), F32)
        blk3 = pl.BlockSpec((tm * N_HEADS, HEAD_W), row)
    else:
        t, n_pre = seq_prefix
        assert t % tm == 0
        tiles = t // tm
        h3o = jax.ShapeDtypeStruct((r // t * (t + n_pre) * N_HEADS, HEAD_W), F32)
        blk3 = pl.BlockSpec(
            (pl.Element(tm * N_HEADS), pl.Element(HEAD_W)),
            lambda i: (((i // tiles) * (t + n_pre) + n_pre + (i % tiles) * tm) * N_HEADS, 0))
    return pl.pallas_call(
        _inproj_kernel,
        grid=(r // tm,),
        in_specs=[blk, pl.BlockSpec((1, d), lambda i: (0, 0)),
                  pl.BlockSpec(w_bf.shape, lambda i: (0, 0), pipeline_mode=pl.Buffered(1))],
        out_specs=[blk, blk, blk, blk3, blk3, blk, blk, blk, blk],
        out_shape=[bfo, bfo, bfo, h3o, h3o, f32o, f32o, f32o, f32o],
        compiler_params=_cparams(("parallel",)),
        name="inproj",
    )(x, g, w_bf)


def _fill_prefix_kernel(m_ref, buf_ref, o_ref):
    del buf_ref
    o_ref[...] = m_ref[...]


def _fill_prefix(m3, buf, n_seq):
    pre = m3.shape[0]
    stride = buf.shape[0] // n_seq
    assert stride % pre == 0
    return pl.pallas_call(
        _fill_prefix_kernel,
        grid=(n_seq,),
        in_specs=[pl.BlockSpec(m3.shape, lambda s: (0, 0)), pl.BlockSpec(memory_space=pl.ANY)],
        out_specs=pl.BlockSpec(m3.shape, lambda s: (s * (stride // pre), 0)),
        out_shape=jax.ShapeDtypeStruct(buf.shape, buf.dtype),
        input_output_aliases={1: 0},
        compiler_params=_cparams(("arbitrary",)),
        name="fill_prefix",
    )(m3, buf)


def _attn_prompt_kernel(lv_ref, q_ref, k_ref, v_ref, km_ref, vm_ref, g_ref, o_ref,
                        qq_scr, s_scr, p_scr, m_scr, l_scr, a_scr, acc_scr, *, tq, hp, rb):
    qi = pl.program_id(2)
    rows = 2 * tq
    nc = tq // LANES
    lam = _lam_from(lv_ref[...])
    lane = lax.broadcasted_iota(I32, (tq, HEAD_W), 1)
    heads = [slice(h * HEAD_W, (h + 1) * HEAD_W) for h in range(hp)]
    wide = lambda x: jnp.broadcast_to(x, (x.shape[0], LANES))

    def scores(qq, kc):
        return lax.dot_general(qq, kc, (((1,), (1,)), ((), ())), preferred_element_type=F32)

    for h, sl in enumerate(heads):
        q = q_ref[0, :, sl]
        zero = jnp.zeros_like(q)
        qq_scr[h, :tq, :] = jnp.where(lane < MAP_W, q, zero)
        qq_scr[h, tq:, :] = jnp.where(lane >= MAP_W, q, zero)
        s0 = scores(qq_scr[h], km_ref[:, sl])
        m = jnp.max(s0, axis=-1, keepdims=True)
        p0 = jnp.exp(s0 - m)
        m_scr[h] = wide(m)
        l_scr[h] = wide(jnp.sum(p0, axis=-1, keepdims=True))
        acc_scr[h] = jnp.dot(p0.astype(BF16), vm_ref[:, sl], preferred_element_type=F32)

    def key_tile(h, sl, off, causal):
        s_scr[h] = scores(qq_scr[h], k_ref[0, pl.ds(off, tq), sl])
        for r in range(rows // rb):
            rs = slice(r * rb, (r + 1) * rb)
            cols = [s_scr[h, rs, c * LANES:(c + 1) * LANES] for c in range(nc)]
            if causal:
                qpos = lax.broadcasted_iota(I32, (rb, LANES), 0) + (r * rb) % tq
                kpos = lax.broadcasted_iota(I32, (rb, LANES), 1)
                cols = [jnp.where(kpos + c * LANES <= qpos, s, -jnp.inf) for c, s in enumerate(cols)]
            mx = cols[0]
            for s in cols[1:]:
                mx = jnp.maximum(mx, s)
            m_prev = m_scr[h, rs, :]
            m_new = jnp.maximum(m_prev, jnp.max(mx, axis=-1, keepdims=True))
            alpha = jnp.exp(m_prev - m_new)
            psum = None
            for c, s in enumerate(cols):
                p = jnp.exp(s - m_new)
                psum = p if psum is None else psum + p
                p_scr[h, rs, c * LANES:(c + 1) * LANES] = p.astype(BF16)
            l_scr[h, rs, :] = alpha * l_scr[h, rs, :] + jnp.sum(psum, axis=-1, keepdims=True)
            m_scr[h, rs, :] = m_new
            a_scr[h, rs, :] = alpha
        acc_scr[h] = a_scr[h] * acc_scr[h] + jnp.dot(p_scr[h], v_ref[0, pl.ds(off, tq), sl],
                                                       preferred_element_type=F32)

    def body(j, c):
        off = pl.multiple_of(j * tq, tq)
        for h, sl in enumerate(heads):
            key_tile(h, sl, off, False)
        return c

    lax.fori_loop(0, qi, body, 0)

    off = pl.multiple_of(qi * tq, tq)
    for h, sl in enumerate(heads):
        key_tile(h, sl, off, True)
        o = acc_scr[h] / l_scr[h]
        o = o[:tq] - lam * o[tq:]
        o_ref[0, :, sl] = _subln(o, g_ref[...]).astype(o_ref.dtype)


def _attn_prompt(lv, qb, kb, vb, kmb, vmb, subln_g):
    b, t, d = qb.shape
    tq = _tile(t, ATT_TILE)
    hp = ATT_HEADS
    w = hp * HEAD_W
    assert tq % LANES == 0 and tq % ATT_STRIP == 0
    rows = 2 * tq
    kern = functools.partial(_attn_prompt_kernel, tq=tq, hp=hp, rb=ATT_STRIP)
    stat = pltpu.VMEM((hp, rows, LANES), F32)
    return pl.pallas_call(
        kern,
        grid=(b, N_HEADS // hp, t // tq),
        scratch_shapes=[pltpu.VMEM((hp, rows, HEAD_W), BF16), pltpu.VMEM((hp, rows, tq), F32),
                        pltpu.VMEM((hp, rows, tq), BF16), stat, stat, stat, stat],
        in_specs=[pl.BlockSpec(lv.shape, lambda bi, h, i: (0, 0)),
                  pl.BlockSpec((1, tq, w), lambda bi, h, i: (bi, i, h)),
                  pl.BlockSpec((1, t, w), lambda bi, h, i: (bi, 0, h)),
                  pl.BlockSpec((1, t, w), lambda bi, h, i: (bi, 0, h)),
                  pl.BlockSpec((N_META, w), lambda bi, h, i: (0, h)),
                  pl.BlockSpec((N_META, w), lambda bi, h, i: (0, h)),
                  pl.BlockSpec((1, HEAD_W), lambda bi, h, i: (0, 0))],
        out_specs=pl.BlockSpec((1, tq, w), lambda bi, h, i: (bi, i, h)),
        out_shape=jax.ShapeDtypeStruct((b, t, d), BF16),
        compiler_params=_cparams(("parallel", "parallel", "arbitrary")),
        name="attn_prompt",
    )(lv, qb, kb, vb, kmb, vmb, subln_g)


def _attn_sample_kernel(pt_ref, lv_ref, q_ref, kn_ref, vn_ref, g_ref, *rest, pp, s_new):
    kpages = rest[:pp]
    vpages = rest[pp:2 * pp]
    o_ref = rest[2 * pp]
    qq_scr, bias_scr, m_scr, l_scr, acc_scr = rest[2 * pp + 1:]
    g = pl.program_id(1)
    ng = pl.num_programs(1)
    hr = 2 * s_new
    rows = N_HEADS * hr

    def head_bias(shape):
        row = lax.broadcasted_iota(I32, shape, 0)
        col = lax.broadcasted_iota(I32, shape, 1)
        return row, col, (col % N_HEADS) == (row // hr)

    @pl.when(g == 0)
    def _():
        lane = lax.broadcasted_iota(I32, (s_new, HEAD_W), 1)
        parts = []
        for h in range(N_HEADS):
            qh = q_ref[0, :, h * HEAD_W:(h + 1) * HEAD_W]
            parts += [jnp.where(lane < MAP_W, qh, 0.0), jnp.where(lane >= MAP_W, qh, 0.0)]
        qq_scr[...] = jnp.concatenate(parts, axis=0).astype(BF16)
        _, _, same = head_bias(bias_scr.shape)
        bias_scr[...] = jnp.where(same, 0.0, -jnp.inf)
        m_scr[...] = jnp.full(m_scr.shape, -jnp.inf, F32)
        l_scr[...] = jnp.zeros(l_scr.shape, F32)
        acc_scr[...] = jnp.zeros(acc_scr.shape, F32)

    def scores(kc):
        return lax.dot_general(qq_scr[...], kc, (((1,), (1,)), ((), ())), preferred_element_type=F32)

    def update(a, s_list, v_list):
        m_prev = m_scr[a]
        s_max = s_list[0]
        for s in s_list[1:]:
            s_max = jnp.maximum(s_max, s)
        m_new = jnp.maximum(m_prev, jnp.max(s_max, axis=-1, keepdims=True))
        alpha = jnp.exp(m_prev - m_new)
        l_add = None
        pv = None
        for s, vc in zip(s_list, v_list):
            p = jnp.exp(s - m_new[:, :1])
            ps = jnp.sum(p, axis=-1, keepdims=True)
            l_add = ps if l_add is None else l_add + ps
            d = jnp.dot(p.astype(BF16), vc, preferred_element_type=F32)
            pv = d if pv is None else pv + d
        m_scr[a] = m_new
        l_scr[a] = alpha * l_scr[a] + l_add
        acc_scr[a] = alpha * acc_scr[a] + pv

    n_acc = m_scr.shape[0]
    per = pp // n_acc
    for a in range(n_acc):
        update(a, [scores(kp[...].astype(BF16)) + bias_scr[...] for kp in kpages[a * per:(a + 1) * per]],
               [vp[...].astype(BF16) for vp in vpages[a * per:(a + 1) * per]])

    @pl.when(g == ng - 1)
    def _():
        lam = _lam_from(lv_ref[...])
        row, col, same = head_bias((rows, s_new * N_HEADS))
        causal = (col // N_HEADS) <= ((row % hr) % s_new)
        s = jnp.where(same & causal, scores(kn_ref[0].astype(BF16)), -jnp.inf)
        update(0, [s], [vn_ref[0].astype(BF16)])
        m = m_scr[0]
        for a in range(1, n_acc):
            m = jnp.maximum(m, m_scr[a])
        l = jnp.zeros_like(m)
        acc = jnp.zeros(acc_scr.shape[1:], F32)
        for a in range(n_acc):
            w = jnp.exp(m_scr[a] - m)
            l = l + w * l_scr[a]
            acc = acc + w * acc_scr[a]
        o = acc / l
        for h in range(N_HEADS):
            oh = o[h * hr:h * hr + s_new] - lam * o[h * hr + s_new:(h + 1) * hr]
            o_ref[0, :, h * HEAD_W:(h + 1) * HEAD_W] = _subln(oh, g_ref[...])


def _attn_sample(page_table, lv, q, kn, vn, subln_g, cache_k, cache_v):
    bd, s_new, d = q.shape
    n_pages = page_table.shape[1]
    page_rows = cache_k.shape[1]
    pp = _tile(n_pages, PAGES_PER_STEP)
    n_acc = _tile(pp, SOFTMAX_SETS)
    rows = N_HEADS * 2 * s_new
    pt = page_table.reshape(-1)
    seq = lambda b, g, pt: (b, 0, 0)

    def page_spec(p):
        return pl.BlockSpec((None, page_rows, HEAD_W),
                            lambda b, g, pt, p=p: (pt[b * n_pages + g * pp + p], 0, 0))

    grid_spec = pltpu.PrefetchScalarGridSpec(
        num_scalar_prefetch=1,
        grid=(bd, n_pages // pp),
        in_specs=[pl.BlockSpec(lv.shape, lambda b, g, pt: (0, 0)),
                  pl.BlockSpec((1, s_new, d), seq),
                  pl.BlockSpec((1, s_new * N_HEADS, HEAD_W), seq),
                  pl.BlockSpec((1, s_new * N_HEADS, HEAD_W), seq),
                  pl.BlockSpec((1, HEAD_W), lambda b, g, pt: (0, 0))]
                 + [page_spec(p) for p in range(pp)] + [page_spec(p) for p in range(pp)],
        out_specs=pl.BlockSpec((1, s_new, d), seq),
        scratch_shapes=[pltpu.VMEM((rows, HEAD_W), BF16),
                        pltpu.VMEM((rows, page_rows), F32),
                        pltpu.VMEM((n_acc, rows, LANES), F32),
                        pltpu.VMEM((n_acc, rows, LANES), F32),
                        pltpu.VMEM((n_acc, rows, HEAD_W), F32)],
    )
    kern = functools.partial(_attn_sample_kernel, pp=pp, s_new=s_new)
    return pl.pallas_call(
        kern,
        grid_spec=grid_spec,
        out_shape=jax.ShapeDtypeStruct((bd, s_new, d), F32),
        compiler_params=_cparams(("parallel", "arbitrary")),
        name="attn_sample",
    )(pt, lv, q, kn, vn, subln_g, *([cache_k] * pp), *([cache_v] * pp))


def _lru_kernel(xr_ref, yr_ref, cb0_ref, h0_ref, cw_ref, cb_ref, wa_ref, ba_ref, wx_ref, bx_ref,
                lam_ref, hg_ref, hl_ref, xbuf, a_scr, u_scr, h_scr, *, n_steps):
    gb, tt, w = xr_ref.shape
    bw = w // LRU_BLOCKS
    i = pl.program_id(1)
    halo = SUBLANES

    @pl.when(i == 0)
    def _():
        xbuf[:, halo - (CONV_W - 1):halo, :] = cb0_ref[...]
        h_scr[...] = h0_ref[...]

    xbuf[:, halo:halo + tt, :] = xr_ref[...]
    xc = cb_ref[...] + xbuf[:, halo - 3:halo - 3 + tt, :] * cw_ref[0:1, :]
    for j in range(1, CONV_W):
        xc = xc + xbuf[:, halo - 3 + j:halo - 3 + j + tt, :] * cw_ref[j:j + 1, :]
    xbuf[:, halo - (CONV_W - 1):halo, :] = xbuf[:, halo + tt - (CONV_W - 1):halo + tt, :]

    xc2 = xc.reshape(gb * tt, w)
    xcb = xc2.astype(BF16)
    r_parts, i_parts = [], []
    for c in range(LRU_BLOCKS):
        xs = xcb[:, c * bw:(c + 1) * bw]
        r_parts.append(jnp.dot(xs, wa_ref[c], preferred_element_type=F32))
        i_parts.append(jnp.dot(xs, wx_ref[c], preferred_element_type=F32))
    r = _sigmoid(jnp.concatenate(r_parts, axis=-1) + ba_ref[...])
    ig = _sigmoid(jnp.concatenate(i_parts, axis=-1) + bx_ref[...])
    nl = -lam_ref[...]
    softplus = jnp.maximum(nl, 0.0) + jnp.log1p(jnp.exp(-jnp.abs(nl)))
    log_a = (-LRU_C * r) * softplus
    a_scr[...] = jnp.exp(log_a).reshape(gb, tt, w)
    mult = jnp.sqrt(-_expm1(2.0 * log_a))
    u_scr[...] = (xc2 * ig * mult).reshape(gb, tt, w)

    def step(t, h):
        h = a_scr[:, t, :] * h + u_scr[:, t, :]
        u_scr[:, t, :] = h
        return h

    h = lax.fori_loop(0, n_steps, step, h_scr[...])
    h_scr[...] = h
    hl_ref[...] = h
    y = yr_ref[...]
    gelu = y * (0.5 * (1.0 + jnp.tanh(math.sqrt(2.0 / math.pi) * (y + 0.044715 * (y * y * y)))))
    hg_ref[...] = (u_scr[...] * gelu).astype(hg_ref.dtype)


def _lru(xr, yr, cb0, h0, p, tt, n_steps):
    b, t, w = xr.shape
    gb = LRU_GB
    assert b % gb == 0 and t % tt == 0 and (n_steps == tt or t == tt)
    bw = w // LRU_BLOCKS
    blk = pl.BlockSpec((gb, tt, w), lambda g, i: (g, i, 0))
    vec = pl.BlockSpec((1, w), lambda g, i: (0, 0))
    wsp = pl.BlockSpec((LRU_BLOCKS, bw, bw), lambda g, i: (0, 0, 0))
    kern = functools.partial(_lru_kernel, n_steps=n_steps)
    return pl.pallas_call(
        kern,
        grid=(b // gb, t // tt),
        in_specs=[blk, blk,
                  pl.BlockSpec((gb, CONV_W - 1, w), lambda g, i: (g, 0, 0)),
                  pl.BlockSpec((gb, w), lambda g, i: (g, 0)),
                  pl.BlockSpec((CONV_W, w), lambda g, i: (0, 0)), vec, wsp, vec, wsp, vec, vec],
        out_specs=[blk, pl.BlockSpec((gb, w), lambda g, i: (g, 0))],
        out_shape=[jax.ShapeDtypeStruct((b, t, w), BF16), jax.ShapeDtypeStruct((b, w), F32)],
        scratch_shapes=[pltpu.VMEM((gb, tt + SUBLANES, w), F32), pltpu.VMEM((gb, tt, w), F32),
                        pltpu.VMEM((gb, tt, w), F32), pltpu.VMEM((gb, w), F32)],
        compiler_params=_cparams(("parallel", "arbitrary")),
        name="lru",
    )(xr, yr, cb0, h0, p["conv_w"], p["conv_b"], p["wa"], p["ba"], p["wx"], p["bx"], p["lam"])


def _post_kernel(o_ref, hg_ref, ga_ref, gr_ref, x_ref, wap_ref, wlp_ref, wo_ref, g2_ref, rw_ref,
                 rb_ref, cin_ref, h1_ref, xn_ref, ei_ref, gt_ref, rk_ref, cnt_ref, carry):
    tm = x_ref.shape[0]

    @pl.when(pl.program_id(0) == 0)
    def _():
        carry[...] = cin_ref[...]

    attn_out = jnp.dot(o_ref[...], wap_ref[...], preferred_element_type=F32)
    lru_out = jnp.dot(hg_ref[...], wlp_ref[...], preferred_element_type=F32)
    merged = _sigmoid(ga_ref[...]) * attn_out + _sigmoid(gr_ref[...]) * lru_out
    h1 = x_ref[...] + jnp.dot(merged.astype(BF16), wo_ref[...], preferred_element_type=F32)
    h1_ref[...] = h1
    ms = jnp.mean(h1 * h1, axis=-1, keepdims=True)
    xn = (h1 * lax.rsqrt(ms + EPS)) * g2_ref[...]
    _store_row_tiles(xn_ref, xn)

    lane = lax.broadcasted_iota(I32, (tm, LANES), 1)
    logits = jnp.dot(xn.astype(BF16), rw_ref[...], preferred_element_type=F32) + rb_ref[...]
    lg = jnp.where(lane < N_EXPERTS, logits, NEG_BIG)
    tops, idxs, hots = [], [], []
    for _ in range(TOP_K):
        mx = jnp.max(lg, axis=-1, keepdims=True)
        idx = jnp.min(jnp.where(lg == mx, lane, LANES), axis=-1, keepdims=True)
        hot = lane == idx
        tops.append(mx); idxs.append(idx); hots.append(hot)
        lg = jnp.where(hot, NEG_BIG, lg)
    exps = [jnp.exp(tv - tops[0]) for tv in tops]
    den = exps[0] + exps[1] + exps[2] + exps[3]

    sel = jnp.zeros((tm, LANES), F32)
    for hot in hots:
        sel = sel + jnp.where(hot, 1.0, 0.0)
    r_i = lax.broadcasted_iota(I32, (tm, tm), 0)
    c_i = lax.broadcasted_iota(I32, (tm, tm), 1)
    ltri = jnp.where(c_i < r_i, 1.0, 0.0).astype(BF16)
    base = jnp.dot(ltri, sel.astype(BF16), preferred_element_type=F32) + carry[...]
    new_carry = carry[...] + jnp.sum(sel, axis=0, keepdims=True)
    carry[...] = new_carry
    cnt_ref[...] = new_carry

    ei = jnp.zeros((tm, LANES), I32)
    gt = jnp.zeros((tm, LANES), F32)
    rk = jnp.zeros((tm, LANES), I32)
    for k in range(TOP_K):
        rank = jnp.sum(jnp.where(hots[k], base, 0.0), axis=-1, keepdims=True)
        ei = jnp.where(lane == k, idxs[k], ei)
        gt = jnp.where(lane == k, exps[k] / den, gt)
        rk = jnp.where(lane == k, rank.astype(I32), rk)
    ei_ref[...] = ei
    gt_ref[...] = gt
    rk_ref[...] = rk


def _post(o, hg, ga, gr, x, p, cnt_in, tm):
    r, d = x.shape
    assert r % tm == 0
    row = lambda i: (i, 0)
    fix = lambda i: (0, 0)
    blk = pl.BlockSpec((tm, d), row)
    wsp = pl.BlockSpec((d, d), fix, pipeline_mode=pl.Buffered(1))
    lsp = pl.BlockSpec((tm, LANES), row)
    vec = pl.BlockSpec((1, LANES), fix)
    return pl.pallas_call(
        _post_kernel,
        grid=(r // tm,),
        in_specs=[blk, blk, blk, blk, blk, wsp, wsp, wsp, pl.BlockSpec((1, d), fix),
                  pl.BlockSpec((d, LANES), fix), vec, vec],
        out_specs=[blk, pl.BlockSpec((tm * SUBLANES, LANES), row), lsp, lsp, lsp, vec],
        out_shape=[jax.ShapeDtypeStruct((r, d), F32), jax.ShapeDtypeStruct((r * SUBLANES, LANES), F32),
                   jax.ShapeDtypeStruct((r, LANES), I32), jax.ShapeDtypeStruct((r, LANES), F32),
                   jax.ShapeDtypeStruct((r, LANES), I32), jax.ShapeDtypeStruct((1, LANES), F32)],
        scratch_shapes=[pltpu.VMEM((1, LANES), F32)],
        compiler_params=_cparams(("arbitrary",)),
        name="post_router",
    )(o, hg, ga, gr, x, p["w_attn_proj"], p["w_lru_proj"], p["w_out"], p["norm2_g"],
      p["router_w"], p["router_b"], cnt_in)


def _dispatch_kernel(e_ref, rk_ref, ps_ref, xn_ref, rows_in_ref, rows_ref, sem, *, tm):
    del rows_in_ref

    def row_copy(j, d):
        return pltpu.make_async_copy(xn_ref.at[j], rows_ref.at[d], sem)

    def issue(jj, c):
        for u in range(DMA_UNROLL):
            j = jj * DMA_UNROLL + u
            for k in range(TOP_K):
                n = j * TOP_K + k
                row_copy(j, ps_ref[e_ref[n]] + rk_ref[n]).start()
        return c

    lax.fori_loop(0, tm // DMA_UNROLL, issue, 0)
    for _ in range(TOP_K):
        pltpu.make_async_copy(xn_ref, rows_ref.at[pl.ds(0, tm)], sem).wait()


def _dispatch(e_flat, rk_flat, pstart, xn, rows, tm):
    n = xn.shape[0]
    assert n % tm == 0
    smem = lambda shape, imap: pl.BlockSpec(shape, imap, memory_space=pltpu.SMEM)
    kern = functools.partial(_dispatch_kernel, tm=tm)
    return pl.pallas_call(
        kern,
        grid=(n // tm,),
        in_specs=[smem((tm * TOP_K,), lambda i: (i,)), smem((tm * TOP_K,), lambda i: (i,)),
                  smem(pstart.shape, lambda i: (0,)),
                  pl.BlockSpec((tm, SUBLANES, LANES), lambda i: (i, 0, 0)), pl.BlockSpec(memory_space=pl.ANY)],
        out_specs=pl.BlockSpec(memory_space=pl.ANY),
        out_shape=jax.ShapeDtypeStruct(rows.shape, rows.dtype),
        scratch_shapes=[pltpu.SemaphoreType.DMA(())],
        input_output_aliases={4: 0},
        compiler_params=_cparams(("arbitrary",)),
        name="moe_dispatch",
    )(e_flat, rk_flat, pstart, xn, rows)


def _expert_kernel(be_ref, nu_ref, x_ref, wg_ref, bg_ref, wu_ref, bu_ref, wd_ref, bd_ref, y_ref,
                   wg_s, wu_s, wd_s):
    j = pl.program_id(0)
    prev = be_ref[jnp.maximum(j - 1, 0)]

    @pl.when((j == 0) | (be_ref[j] != prev))
    def _():
        wg_s[...] = wg_ref[0].astype(BF16)
        wu_s[...] = wu_ref[0].astype(BF16)
        wd_s[...] = wd_ref[0].astype(BF16)

    @pl.when(j < nu_ref[0])
    def _():
        tm = x_ref.shape[0] // SUBLANES
        xb = _load_row_tiles(x_ref, tm).astype(BF16)
        g = jnp.dot(xb, wg_s[...], preferred_element_type=F32) + bg_ref[0]
        up = jnp.dot(xb, wu_s[...], preferred_element_type=F32) + bu_ref[0]
        g = jnp.minimum(g, SWIGLU_LIMIT)
        up = jnp.clip(up, -SWIGLU_LIMIT, SWIGLU_LIMIT)
        h = (up + 1.0) * (g * _sigmoid(SWIGLU_ALPHA * g))
        _store_row_tiles(y_ref, jnp.dot(h.astype(BF16), wd_s[...], preferred_element_type=F32) + bd_ref[0])


def _experts(blk_e, n_used, rows, p, tm):
    n_rows = rows.shape[0] // SUBLANES
    d = SUBLANES * LANES
    dff = p["w_gate"].shape[2]
    n_blk = n_rows // tm
    wsp = lambda shape: pl.BlockSpec((1,) + shape, lambda j, be, nu: (be[j], 0, 0))
    grid_spec = pltpu.PrefetchScalarGridSpec(
        num_scalar_prefetch=2,
        grid=(n_blk,),
        in_specs=[pl.BlockSpec((tm * SUBLANES, LANES), lambda j, be, nu: (j, 0)),
                  wsp((d, dff)), wsp((1, dff)), wsp((d, dff)), wsp((1, dff)),
                  wsp((dff, d)), wsp((1, d))],
        out_specs=pl.BlockSpec((tm * SUBLANES, LANES), lambda j, be, nu: (j, 0)),
        scratch_shapes=[pltpu.VMEM((d, dff), BF16), pltpu.VMEM((d, dff), BF16),
                        pltpu.VMEM((dff, d), BF16)],
    )
    return pl.pallas_call(
        _expert_kernel,
        grid_spec=grid_spec,
        out_shape=jax.ShapeDtypeStruct((n_rows * SUBLANES, LANES), F32),
        compiler_params=_cparams(("arbitrary",)),
        name="moe_experts",
    )(blk_e, n_used, rows, p["w_gate"], p["b_gate"], p["w_up"], p["b_up"], p["w_down"], p["b_down"])


def _combine_kernel(e_ref, rk_ref, en_ref, rkn_ref, ps_ref, y_ref, gt_ref, h1_ref, gf_ref, out_ref,
                    buf, sems, *, tm):
    i = pl.program_id(0)
    slot = i % 2

    def gather(er, rr, sl):
        def issue(jj, c):
            for u in range(DMA_UNROLL):
                j = jj * DMA_UNROLL + u
                for k in range(TOP_K):
                    n = j * TOP_K + k
                    dst = buf.at[sl, k, pl.ds(pl.multiple_of(j * SUBLANES, SUBLANES), SUBLANES), :]
                    pltpu.make_async_copy(y_ref.at[ps_ref[er[n]] + rr[n]], dst, sems.at[sl]).start()
            return c

        lax.fori_loop(0, tm // DMA_UNROLL, issue, 0)

    @pl.when(i == 0)
    def _():
        gather(e_ref, rk_ref, 0)

    @pl.when(i + 1 < pl.num_programs(0))
    def _():
        gather(en_ref, rkn_ref, 1 - slot)

    def finish(sl):
        for k in range(TOP_K):
            pltpu.make_async_copy(buf.at[sl, k], buf.at[sl, k], sems.at[sl]).wait()
        gt = gt_ref[...]
        y = _load_row_tiles(buf.at[sl, 0], tm) * gt[:, 0:1]
        for k in range(1, TOP_K):
            y = y + _load_row_tiles(buf.at[sl, k], tm) * gt[:, k:k + 1]
        h2 = h1_ref[...] + y
        ms = jnp.mean(h2 * h2, axis=-1, keepdims=True)
        out_ref[...] = (h2 * lax.rsqrt(ms + EPS)) * gf_ref[...]

    for sl in range(2):
        pl.when(slot == sl)(functools.partial(finish, sl))


def _combine(e_flat, rk_flat, pstart, y_rows, gt, h1, gf, tm):
    n, d = h1.shape
    assert n % tm == 0
    nt = n // tm
    smem = lambda shape, imap: pl.BlockSpec(shape, imap, memory_space=pltpu.SMEM)
    row = lambda i: (i, 0)
    cur = lambda i: (i,)
    nxt = lambda i: (jnp.minimum(i + 1, nt - 1),)
    kern = functools.partial(_combine_kernel, tm=tm)
    return pl.pallas_call(
        kern,
        grid=(nt,),
        in_specs=[smem((tm * TOP_K,), cur), smem((tm * TOP_K,), cur),
                  smem((tm * TOP_K,), nxt), smem((tm * TOP_K,), nxt),
                  smem(pstart.shape, lambda i: (0,)),
                  pl.BlockSpec(memory_space=pl.ANY),
                  pl.BlockSpec((tm, LANES), row), pl.BlockSpec((tm, d), row),
                  pl.BlockSpec((1, d), lambda i: (0, 0))],
        out_specs=pl.BlockSpec((tm, d), row),
        out_shape=jax.ShapeDtypeStruct((n, d), F32),
        scratch_shapes=[pltpu.VMEM((2, TOP_K, tm * SUBLANES, LANES), F32), pltpu.SemaphoreType.DMA((2,))],
        compiler_params=_cparams(("arbitrary",)),
        name="moe_combine",
    )(e_flat, rk_flat, e_flat, rk_flat, pstart, y_rows, gt, h1, gf)


def kernel(x_prompt, x_sample, cache_k, cache_v, state_conv, state_h, page_table, meta_tokens, norm1_g, w_in, lambda_q1, lambda_k1, lambda_q2, lambda_k2, subln_g, conv_w, conv_b, rg_wa, rg_ba, rg_wx, rg_bx, rg_lambda, w_attn_proj, w_lru_proj, w_out, norm2_g, router_w, router_b, w_gate, b_gate, w_up, b_up, w_down, b_down, final_norm_g):
    depth = w_in.shape[0]
    assert depth == 1, "single-layer stack"
    b, t, d = x_prompt.shape
    bd, s_new, _ = x_sample.shape
    n_p, n_s = b * t, bd * s_new
    row1 = lambda a: a.reshape(1, -1)

    w_in_bf = w_in[0].astype(BF16)
    g1 = row1(norm1_g[0])
    lv = jnp.stack([lambda_q1[0], lambda_k1[0], lambda_q2[0], lambda_k2[0]])
    sg = row1(subln_g[0])
    lru_p = dict(conv_w=conv_w[0], conv_b=row1(conv_b[0]), wa=rg_wa[0].astype(BF16), ba=row1(rg_ba[0]),
                 wx=rg_wx[0].astype(BF16), bx=row1(rg_bx[0]), lam=row1(rg_lambda[0]))
    rw = jnp.zeros((d, LANES), BF16).at[:, :N_EXPERTS].set(router_w[0].astype(BF16))
    rb = jnp.zeros((1, LANES), F32).at[0, :N_EXPERTS].set(router_b[0])
    post_p = dict(w_attn_proj=w_attn_proj[0].astype(BF16), w_lru_proj=w_lru_proj[0].astype(BF16),
                  w_out=w_out[0].astype(BF16), norm2_g=row1(norm2_g[0]), router_w=rw, router_b=rb)
    exp_p = dict(w_gate=w_gate[0], b_gate=b_gate[0][:, None, :], w_up=w_up[0], b_up=b_up[0][:, None, :],
                 w_down=w_down[0], b_down=b_down[0][:, None, :])

    xp = x_prompt.reshape(n_p, d)
    xs = x_sample.reshape(n_s, d)
    qbP, kbP, vbP, k3P, v3P, xrP, yrP, gaP, grP = _inproj(xp, g1, w_in_bf, _tile(t, ROW_TILE), (t, N_META))
    qbS, kbS, vbS, k3S, v3S, xrS, yrS, gaS, grS = _inproj(xs, g1, w_in_bf, _tile(n_s, ROW_TILE))
    _, kbM, vbM, k3M, v3M, xrM, yrM, _, _ = _inproj(meta_tokens, g1, w_in_bf, N_META)

    oP = _attn_prompt(lv, qbP.reshape(b, t, d), kbP.reshape(b, t, d), vbP.reshape(b, t, d), kbM, vbM, sg)
    n_pool, page = cache_k.shape[1], cache_k.shape[2]
    oS = _attn_sample(page_table, lv, qbS.astype(F32).reshape(bd, s_new, d),
                      k3S.reshape(bd, s_new * N_HEADS, HEAD_W), v3S.reshape(bd, s_new * N_HEADS, HEAD_W), sg,
                      cache_k[0].reshape(n_pool, page * N_HEADS, HEAD_W),
                      cache_v[0].reshape(n_pool, page * N_HEADS, HEAD_W))

    gb = LRU_GB
    zc = jnp.zeros((gb, CONV_W - 1, d), F32)
    zh = jnp.zeros((gb, d), F32)
    bc = lambda a: jnp.broadcast_to(a[None], (gb,) + a.shape)
    _, hM = _lru(bc(xrM), bc(yrM), zc, zh, lru_p, N_META, N_META)
    cbP = jnp.broadcast_to(xrM[None, N_META - (CONV_W - 1):], (b, CONV_W - 1, d))
    h0P = jnp.broadcast_to(hM[0:1], (b, d))
    xrP3 = xrP.reshape(b, t, d)
    hgP, hlP = _lru(xrP3, yrP.reshape(b, t, d), cbP, h0P, lru_p, _tile(t, LRU_TT), _tile(t, LRU_TT))
    t_pad = SUBLANES
    padt = lambda a: jnp.pad(a.reshape(bd, s_new, d), ((0, 0), (0, t_pad - s_new), (0, 0)))
    xrS3 = xrS.reshape(bd, s_new, d)
    hgS, hlS = _lru(padt(xrS), padt(yrS), state_conv[0], state_h[0], lru_p, t_pad, s_new)
    hgS = hgS[:, :s_new].reshape(n_s, d)

    cnt0 = jnp.zeros((1, LANES), F32)
    h1P, xnP, eiP, gtP, rkP, cntP = _post(oP.reshape(n_p, d), hgP.reshape(n_p, d), gaP, grP, xp, post_p,
                                          cnt0, _tile(n_p, ROW_TILE))
    h1S, xnS, eiS, gtS, rkS, cnt = _post(oS.reshape(n_s, d).astype(BF16), hgS, gaS, grS, xs, post_p,
                                         cntP, _tile(n_s, ROW_TILE))

    n_tok = n_p + n_s
    tmx = MOE_TILE
    counts = cnt[0, :N_EXPERTS].astype(I32)
    padded = (counts + tmx - 1) // tmx * tmx
    pend = jnp.cumsum(padded)
    pstart = (pend - padded).astype(I32)
    n_rows = -(-(n_tok * TOP_K + N_EXPERTS * (tmx - 1)) // tmx) * tmx
    n_blk = n_rows // tmx
    blk_start = jnp.arange(n_blk, dtype=I32) * tmx
    blk_e = jnp.minimum(jnp.sum((pend[None, :] <= blk_start[:, None]).astype(I32), axis=1), N_EXPERTS - 1)
    n_used = (pend[-1:] // tmx).astype(I32)
    flat = lambda a: a[:, :TOP_K].reshape(-1)

    tiles = lambda a: a.reshape(-1, SUBLANES, LANES)
    rows = jnp.zeros((n_rows, SUBLANES, LANES), F32)
    rows = _dispatch(flat(eiP), flat(rkP), pstart, tiles(xnP), rows, _tile(n_p, DISPATCH_TILE))
    rows = _dispatch(flat(eiS), flat(rkS), pstart, tiles(xnS), rows, _tile(n_s, DISPATCH_TILE))
    y_rows = tiles(_experts(blk_e, n_used, rows.reshape(-1, LANES), exp_p, tmx))
    gf = row1(final_norm_g)
    yP = _combine(flat(eiP), flat(rkP), pstart, y_rows, gtP, h1P, gf, _tile(n_p, COMBINE_TILE))
    yS = _combine(flat(eiS), flat(rkS), pstart, y_rows, gtS, h1S, gf, _tile(n_s, COMBINE_TILE))

    def with_meta(x3, m3):
        return _fill_prefix(m3, x3, b).reshape(1, b, t + N_META, N_HEADS, HEAD_W)

    nc = CONV_W - 1
    return (yP.reshape(b, t, d), yS.reshape(bd, s_new, d),
            with_meta(k3P, k3M), with_meta(v3P, v3M),
            xrP3[:, t - nc:][None], hlP[None],
            k3S.reshape(1, bd, s_new, N_HEADS, HEAD_W), v3S.reshape(1, bd, s_new, N_HEADS, HEAD_W),
            xrS3[:, s_new - nc:][None], hlS[None])
```

```python
import functools
import math

import jax
import jax.numpy as jnp
from jax import lax
from jax.experimental import pallas as pl
from jax.experimental.pallas import tpu as pltpu

F32 = jnp.float32
BF16 = jnp.bfloat16
I32 = jnp.int32

LANES = 128
SUBLANES = 8
N_HEADS = 8
HEAD_W = 128
MAP_W = 64
N_META = 16
CONV_W = 4
LRU_BLOCKS = 4
LRU_C = 8.0
N_EXPERTS = 32
TOP_K = 4
SWIGLU_LIMIT = 7.0
SWIGLU_ALPHA = 1.702
EPS = 1e-6
SUBLN_EPS = 1e-5
LAM_INIT = 0.8 - 0.6 * math.exp(-0.3 * 0)
NEG_BIG = -1e30
VMEM_LIMIT = 56 * 1024 * 1024

ROW_TILE = 256
MOE_TILE = 512
ATT_TILE = 512
ATT_HEADS = 2
ATT_STRIP = 32
LRU_TT = 64
LRU_GB = 8
PAGES_PER_STEP = 16
SOFTMAX_SETS = 4
DISPATCH_TILE = 512
COMBINE_TILE = 128
DMA_UNROLL = 4


def _cparams(sem):
    return pltpu.CompilerParams(dimension_semantics=sem, vmem_limit_bytes=VMEM_LIMIT)


def _tile(n, pref):
    t = min(pref, n)
    while n % t:
        t //= 2
    return t


def _store_row_tiles(ref, val):
    n = val.shape[0]
    for s in range(SUBLANES):
        ref[pl.ds(s, n, stride=SUBLANES), :] = val[:, s * LANES:(s + 1) * LANES]


def _load_row_tiles(ref, n):
    return jnp.concatenate([ref[pl.ds(s, n, stride=SUBLANES), :] for s in range(SUBLANES)], axis=-1)


def _sigmoid(x):
    return 1.0 / (1.0 + jnp.exp(-x))


def _expm1(x):
    u = jnp.exp(x)
    um1 = u - 1.0
    return jnp.where(u == 1.0, x, jnp.where(um1 == -1.0, -1.0, um1 * x / jnp.log(u)))


def _lam_from(lv):
    a = jnp.sum(lv[0:1, :] * lv[1:2, :], axis=-1, keepdims=True)
    b = jnp.sum(lv[2:3, :] * lv[3:4, :], axis=-1, keepdims=True)
    return jnp.exp(a) - jnp.exp(b) + LAM_INIT


def _subln(o, g):
    ms = jnp.mean(o * o, axis=-1, keepdims=True)
    return ((o * lax.rsqrt(ms + SUBLN_EPS)) * g) * (1.0 - LAM_INIT)


def _inproj_kernel(x_ref, g_ref, w_ref, qb_ref, kb_ref, vb_ref, k3_ref, v3_ref,
                   xr_ref, yr_ref, ga_ref, gr_ref):
    tm, d = x_ref.shape
    x = x_ref[...]
    ms = jnp.mean(x * x, axis=-1, keepdims=True)
    ub = ((x * lax.rsqrt(ms + EPS)) * g_ref[...]).astype(BF16)

    def proj(j):
        return jnp.dot(ub, w_ref[:, j * d:(j + 1) * d], preferred_element_type=F32)

    q = proj(0)
    qb_ref[...] = (q * (MAP_W ** -0.5)).astype(BF16)
    k = proj(1)
    kb_ref[...] = k.astype(BF16)
    _store_row_tiles(k3_ref, k)
    v = proj(2)
    vb_ref[...] = v.astype(BF16)
    _store_row_tiles(v3_ref, v)
    xr_ref[...] = proj(3)
    yr_ref[...] = proj(4)
    ga_ref[...] = proj(5)
    gr_ref[...] = proj(6)


def _inproj(x, g, w_bf, tm, seq_prefix=None):
    r, d = x.shape
    assert r % tm == 0
    row = lambda i: (i, 0)
    f32o = jax.ShapeDtypeStruct((r, d), F32)
    bfo = jax.ShapeDtypeStruct((r, d), BF16)
    blk = pl.BlockSpec((tm, d), row)
    if seq_prefix is None:
        h3o = jax.ShapeDtypeStruct((r * N_HEADS, HEAD_W), F32)
        blk3 = pl.BlockSpec((tm * N_HEADS, HEAD_W), row)
    else:
        t, n_pre = seq_prefix
        assert t % tm == 0
        tiles = t // tm
        h3o = jax.ShapeDtypeStruct((r // t * (t + n_pre) * N_HEADS, HEAD_W), F32)
        blk3 = pl.BlockSpec(
            (pl.Element(tm * N_HEADS), pl.Element(HEAD_W)),
            lambda i: (((i // tiles) * (t + n_pre) + n_pre + (i % tiles) * tm) * N_HEADS, 0))
    return pl.pallas_call(
        _inproj_kernel,
        grid=(r // tm,),
        in_specs=[blk, pl.BlockSpec((1, d), lambda i: (0, 0)),
                  pl.BlockSpec(w_bf.shape, lambda i: (0, 0), pipeline_mode=pl.Buffered(1))],
        out_specs=[blk, blk, blk, blk3, blk3, blk, blk, blk, blk],
        out_shape=[bfo, bfo, bfo, h3o, h3o, f32o, f32o, f32o, f32o],
        compiler_params=_cparams(("parallel",)),
        name="inproj",
    )(x, g, w_bf)


def _fill_prefix_kernel(m_ref, buf_ref, o_ref):
    del buf_ref
    o_ref[...] = m_ref[...]


def _fill_prefix(m3, buf, n_seq):
    pre = m3.shape[0]
    stride = buf.shape[0] // n_seq
    assert stride % pre == 0
    return pl.pallas_call(
        _fill_prefix_kernel,
        grid=(n_seq,),
        in_specs=[pl.BlockSpec(m3.shape, lambda s: (0, 0)), pl.BlockSpec(memory_space=pl.ANY)],
        out_specs=pl.BlockSpec(m3.shape, lambda s: (s * (stride // pre), 0)),
        out_shape=jax.ShapeDtypeStruct(buf.shape, buf.dtype),
        input_output_aliases={1: 0},
        compiler_params=_cparams(("arbitrary",)),
        name="fill_prefix",
    )(m3, buf)


def _attn_prompt_kernel(lv_ref, q_ref, k_ref, v_ref, km_ref, vm_ref, g_ref, o_ref,
                        qq_scr, s_scr, p_scr, m_scr, l_scr, a_scr, acc_scr, *, tq, hp, rb):
    qi = pl.program_id(2)
    rows = 2 * tq
    nc = tq // LANES
    lam = _lam_from(lv_ref[...])
    lane = lax.broadcasted_iota(I32, (tq, HEAD_W), 1)
    heads = [slice(h * HEAD_W, (h + 1) * HEAD_W) for h in range(hp)]
    wide = lambda x: jnp.broadcast_to(x, (x.shape[0], LANES))

    def scores(qq, kc):
        return lax.dot_general(qq, kc, (((1,), (1,)), ((), ())), preferred_element_type=F32)

    for h, sl in enumerate(heads):
        q = q_ref[0, :, sl]
        zero = jnp.zeros_like(q)
        qq_scr[h, :tq, :] = jnp.where(lane < MAP_W, q, zero)
        qq_scr[h, tq:, :] = jnp.where(lane >= MAP_W, q, zero)
        s0 = scores(qq_scr[h], km_ref[:, sl])
        m = jnp.max(s0, axis=-1, keepdims=True)
        p0 = jnp.exp(s0 - m)
        m_scr[h] = wide(m)
        l_scr[h] = wide(jnp.sum(p0, axis=-1, keepdims=True))
        acc_scr[h] = jnp.dot(p0.astype(BF16), vm_ref[:, sl], preferred_element_type=F32)

    def key_tile(h, sl, off, causal):
        s_scr[h] = scores(qq_scr[h], k_ref[0, pl.ds(off, tq), sl])
        for r in range(rows // rb):
            rs = slice(r * rb, (r + 1) * rb)
            q0 = (r * rb) % tq
            live = min(nc, (q0 + rb - 1) // LANES + 1) if causal else nc
            cols = [s_scr[h, rs, c * LANES:(c + 1) * LANES] for c in range(live)]
            for c in range(live, nc):
                p_scr[h, rs, c * LANES:(c + 1) * LANES] = jnp.zeros((rb, LANES), BF16)
            if causal:
                qpos = lax.broadcasted_iota(I32, (rb, LANES), 0) + q0
                kpos = lax.broadcasted_iota(I32, (rb, LANES), 1)
                cols = [jnp.where(kpos + c * LANES <= qpos, s, -jnp.inf) for c, s in enumerate(cols)]
            mx = cols[0]
            for s in cols[1:]:
                mx = jnp.maximum(mx, s)
            m_prev = m_scr[h, rs, :]
            m_new = jnp.maximum(m_prev, jnp.max(mx, axis=-1, keepdims=True))
            alpha = jnp.exp(m_prev - m_new)
            psum = None
            for c, s in enumerate(cols):
                p = jnp.exp(s - m_new)
                psum = p if psum is None else psum + p
                p_scr[h, rs, c * LANES:(c + 1) * LANES] = p.astype(BF16)
            l_scr[h, rs, :] = alpha * l_scr[h, rs, :] + jnp.sum(psum, axis=-1, keepdims=True)
            m_scr[h, rs, :] = m_new
            a_scr[h, rs, :] = alpha
        acc_scr[h] = a_scr[h] * acc_scr[h] + jnp.dot(p_scr[h], v_ref[0, pl.ds(off, tq), sl],
                                                       preferred_element_type=F32)

    def body(j, c):
        off = pl.multiple_of(j * tq, tq)
        for h, sl in enumerate(heads):
            key_tile(h, sl, off, False)
        return c

    lax.fori_loop(0, qi, body, 0)

    off = pl.multiple_of(qi * tq, tq)
    for h, sl in enumerate(heads):
        key_tile(h, sl, off, True)
        o = acc_scr[h] / l_scr[h]
        o = o[:tq] - lam * o[tq:]
        o_ref[0, :, sl] = _subln(o, g_ref[...]).astype(o_ref.dtype)


def _attn_prompt(lv, qb, kb, vb, kmb, vmb, subln_g):
    b, t, d = qb.shape
    tq = _tile(t, ATT_TILE)
    hp = ATT_HEADS
    w = hp * HEAD_W
    assert tq % LANES == 0 and tq % ATT_STRIP == 0
    rows = 2 * tq
    kern = functools.partial(_attn_prompt_kernel, tq=tq, hp=hp, rb=ATT_STRIP)
    stat = pltpu.VMEM((hp, rows, LANES), F32)
    return pl.pallas_call(
        kern,
        grid=(b, N_HEADS // hp, t // tq),
        scratch_shapes=[pltpu.VMEM((hp, rows, HEAD_W), BF16), pltpu.VMEM((hp, rows, tq), F32),
                        pltpu.VMEM((hp, rows, tq), BF16), stat, stat, stat, stat],
        in_specs=[pl.BlockSpec(lv.shape, lambda bi, h, i: (0, 0)),
                  pl.BlockSpec((1, tq, w), lambda bi, h, i: (bi, i, h)),
                  pl.BlockSpec((1, t, w), lambda bi, h, i: (bi, 0, h)),
                  pl.BlockSpec((1, t, w), lambda bi, h, i: (bi, 0, h)),
                  pl.BlockSpec((N_META, w), lambda bi, h, i: (0, h)),
                  pl.BlockSpec((N_META, w), lambda bi, h, i: (0, h)),
                  pl.BlockSpec((1, HEAD_W), lambda bi, h, i: (0, 0))],
        out_specs=pl.BlockSpec((1, tq, w), lambda bi, h, i: (bi, i, h)),
        out_shape=jax.ShapeDtypeStruct((b, t, d), BF16),
        compiler_params=_cparams(("parallel", "parallel", "arbitrary")),
        name="attn_prompt",
    )(lv, qb, kb, vb, kmb, vmb, subln_g)


def _attn_sample_kernel(pt_ref, lv_ref, q_ref, kn_ref, vn_ref, g_ref, *rest, pp, s_new):
    kpages = rest[:pp]
    vpages = rest[pp:2 * pp]
    o_ref = rest[2 * pp]
    qq_scr, bias_scr, m_scr, l_scr, acc_scr = rest[2 * pp + 1:]
    g = pl.program_id(1)
    ng = pl.num_programs(1)
    hr = 2 * s_new
    rows = N_HEADS * hr

    def head_bias(shape):
        row = lax.broadcasted_iota(I32, shape, 0)
        col = lax.broadcasted_iota(I32, shape, 1)
        return row, col, (col % N_HEADS) == (row // hr)

    @pl.when(g == 0)
    def _():
        lane = lax.broadcasted_iota(I32, (s_new, HEAD_W), 1)
        parts = []
        for h in range(N_HEADS):
            qh = q_ref[0, :, h * HEAD_W:(h + 1) * HEAD_W]
            parts += [jnp.where(lane < MAP_W, qh, 0.0), jnp.where(lane >= MAP_W, qh, 0.0)]
        qq_scr[...] = jnp.concatenate(parts, axis=0).astype(BF16)
        _, _, same = head_bias(bias_scr.shape)
        bias_scr[...] = jnp.where(same, 0.0, -jnp.inf)
        m_scr[...] = jnp.full(m_scr.shape, -jnp.inf, F32)
        l_scr[...] = jnp.zeros(l_scr.shape, F32)
        acc_scr[...] = jnp.zeros(acc_scr.shape, F32)

    def scores(kc):
        return lax.dot_general(qq_scr[...], kc, (((1,), (1,)), ((), ())), preferred_element_type=F32)

    def update(a, s_list, v_list):
        m_prev = m_scr[a]
        s_max = s_list[0]
        for s in s_list[1:]:
            s_max = jnp.maximum(s_max, s)
        m_new = jnp.maximum(m_prev, jnp.max(s_max, axis=-1, keepdims=True))
        alpha = jnp.exp(m_prev - m_new)
        l_add = None
        pv = None
        for s, vc in zip(s_list, v_list):
            p = jnp.exp(s - m_new[:, :1])
            ps = jnp.sum(p, axis=-1, keepdims=True)
            l_add = ps if l_add is None else l_add + ps
            d = jnp.dot(p.astype(BF16), vc, preferred_element_type=F32)
            pv = d if pv is None else pv + d
        m_scr[a] = m_new
        l_scr[a] = alpha * l_scr[a] + l_add
        acc_scr[a] = alpha * acc_scr[a] + pv

    n_acc = m_scr.shape[0]
    per = pp // n_acc
    for a in range(n_acc):
        update(a, [scores(kp[...].astype(BF16)) + bias_scr[...] for kp in kpages[a * per:(a + 1) * per]],
               [vp[...].astype(BF16) for vp in vpages[a * per:(a + 1) * per]])

    @pl.when(g == ng - 1)
    def _():
        lam = _lam_from(lv_ref[...])
        row, col, same = head_bias((rows, s_new * N_HEADS))
        causal = (col // N_HEADS) <= ((row % hr) % s_new)
        s = jnp.where(same & causal, scores(kn_ref[0].astype(BF16)), -jnp.inf)
        update(0, [s], [vn_ref[0].astype(BF16)])
        m = m_scr[0]
        for a in range(1, n_acc):
            m = jnp.maximum(m, m_scr[a])
        l = jnp.zeros_like(m)
        acc = jnp.zeros(acc_scr.shape[1:], F32)
        for a in range(n_acc):
            w = jnp.exp(m_scr[a] - m)
            l = l + w * l_scr[a]
            acc = acc + w * acc_scr[a]
        o = acc / l
        for h in range(N_HEADS):
            oh = o[h * hr:h * hr + s_new] - lam * o[h * hr + s_new:(h + 1) * hr]
            o_ref[0, :, h * HEAD_W:(h + 1) * HEAD_W] = _subln(oh, g_ref[...])


def _attn_sample(page_table, lv, q, kn, vn, subln_g, cache_k, cache_v):
    bd, s_new, d = q.shape
    n_pages = page_table.shape[1]
    page_rows = cache_k.shape[1]
    pp = _tile(n_pages, PAGES_PER_STEP)
    n_acc = _tile(pp, SOFTMAX_SETS)
    rows = N_HEADS * 2 * s_new
    pt = page_table.reshape(-1)
    seq = lambda b, g, pt: (b, 0, 0)

    def page_spec(p):
        return pl.BlockSpec((None, page_rows, HEAD_W),
                            lambda b, g, pt, p=p: (pt[b * n_pages + g * pp + p], 0, 0))

    grid_spec = pltpu.PrefetchScalarGridSpec(
        num_scalar_prefetch=1,
        grid=(bd, n_pages // pp),
        in_specs=[pl.BlockSpec(lv.shape, lambda b, g, pt: (0, 0)),
                  pl.BlockSpec((1, s_new, d), seq),
                  pl.BlockSpec((1, s_new * N_HEADS, HEAD_W), seq),
                  pl.BlockSpec((1, s_new * N_HEADS, HEAD_W), seq),
                  pl.BlockSpec((1, HEAD_W), lambda b, g, pt: (0, 0))]
                 + [page_spec(p) for p in range(pp)] + [page_spec(p) for p in range(pp)],
        out_specs=pl.BlockSpec((1, s_new, d), seq),
        scratch_shapes=[pltpu.VMEM((rows, HEAD_W), BF16),
                        pltpu.VMEM((rows, page_rows), F32),
                        pltpu.VMEM((n_acc, rows, LANES), F32),
                        pltpu.VMEM((n_acc, rows, LANES), F32),
                        pltpu.VMEM((n_acc, rows, HEAD_W), F32)],
    )
    kern = functools.partial(_attn_sample_kernel, pp=pp, s_new=s_new)
    return pl.pallas_call(
        kern,
        grid_spec=grid_spec,
        out_shape=jax.ShapeDtypeStruct((bd, s_new, d), F32),
        compiler_params=_cparams(("parallel", "arbitrary")),
        name="attn_sample",
    )(pt, lv, q, kn, vn, subln_g, *([cache_k] * pp), *([cache_v] * pp))


def _lru_kernel(xr_ref, yr_ref, cb0_ref, h0_ref, cw_ref, cb_ref, wa_ref, ba_ref, wx_ref, bx_ref,
                lam_ref, hg_ref, hl_ref, xbuf, a_scr, u_scr, h_scr, *, n_steps):
    gb, tt, w = xr_ref.shape
    bw = w // LRU_BLOCKS
    i = pl.program_id(1)
    halo = SUBLANES

    @pl.when(i == 0)
    def _():
        xbuf[:, halo - (CONV_W - 1):halo, :] = cb0_ref[...]
        h_scr[...] = h0_ref[...]

    xbuf[:, halo:halo + tt, :] = xr_ref[...]
    xc = cb_ref[...] + xbuf[:, halo - 3:halo - 3 + tt, :] * cw_ref[0:1, :]
    for j in range(1, CONV_W):
        xc = xc + xbuf[:, halo - 3 + j:halo - 3 + j + tt, :] * cw_ref[j:j + 1, :]
    xbuf[:, halo - (CONV_W - 1):halo, :] = xbuf[:, halo + tt - (CONV_W - 1):halo + tt, :]

    xc2 = xc.reshape(gb * tt, w)
    xcb = xc2.astype(BF16)
    r_parts, i_parts = [], []
    for c in range(LRU_BLOCKS):
        xs = xcb[:, c * bw:(c + 1) * bw]
        r_parts.append(jnp.dot(xs, wa_ref[c], preferred_element_type=F32))
        i_parts.append(jnp.dot(xs, wx_ref[c], preferred_element_type=F32))
    r = _sigmoid(jnp.concatenate(r_parts, axis=-1) + ba_ref[...])
    ig = _sigmoid(jnp.concatenate(i_parts, axis=-1) + bx_ref[...])
    nl = -lam_ref[...]
    softplus = jnp.maximum(nl, 0.0) + jnp.log1p(jnp.exp(-jnp.abs(nl)))
    log_a = (-LRU_C * r) * softplus
    a_scr[...] = jnp.exp(log_a).reshape(gb, tt, w)
    mult = jnp.sqrt(-_expm1(2.0 * log_a))
    u_scr[...] = (xc2 * ig * mult).reshape(gb, tt, w)

    def step(t, h):
        h = a_scr[:, t, :] * h + u_scr[:, t, :]
        u_scr[:, t, :] = h
        return h

    h = lax.fori_loop(0, n_steps, step, h_scr[...])
    h_scr[...] = h
    hl_ref[...] = h
    y = yr_ref[...]
    gelu = y * (0.5 * (1.0 + jnp.tanh(math.sqrt(2.0 / math.pi) * (y + 0.044715 * (y * y * y)))))
    hg_ref[...] = (u_scr[...] * gelu).astype(hg_ref.dtype)


def _lru(xr, yr, cb0, h0, p, tt, n_steps):
    b, t, w = xr.shape
    gb = LRU_GB
    assert b % gb == 0 and t % tt == 0 and (n_steps == tt or t == tt)
    bw = w // LRU_BLOCKS
    blk = pl.BlockSpec((gb, tt, w), lambda g, i: (g, i, 0))
    vec = pl.BlockSpec((1, w), lambda g, i: (0, 0))
    wsp = pl.BlockSpec((LRU_BLOCKS, bw, bw), lambda g, i: (0, 0, 0))
    kern = functools.partial(_lru_kernel, n_steps=n_steps)
    return pl.pallas_call(
        kern,
        grid=(b // gb, t // tt),
        in_specs=[blk, blk,
                  pl.BlockSpec((gb, CONV_W - 1, w), lambda g, i: (g, 0, 0)),
                  pl.BlockSpec((gb, w), lambda g, i: (g, 0)),
                  pl.BlockSpec((CONV_W, w), lambda g, i: (0, 0)), vec, wsp, vec, wsp, vec, vec],
        out_specs=[blk, pl.BlockSpec((gb, w), lambda g, i: (g, 0))],
        out_shape=[jax.ShapeDtypeStruct((b, t, w), BF16), jax.ShapeDtypeStruct((b, w), F32)],
        scratch_shapes=[pltpu.VMEM((gb, tt + SUBLANES, w), F32), pltpu.VMEM((gb, tt, w), F32),
                        pltpu.VMEM((gb, tt, w), F32), pltpu.VMEM((gb, w), F32)],
        compiler_params=_cparams(("parallel", "arbitrary")),
        name="lru",
    )(xr, yr, cb0, h0, p["conv_w"], p["conv_b"], p["wa"], p["ba"], p["wx"], p["bx"], p["lam"])


def _post_kernel(o_ref, hg_ref, ga_ref, gr_ref, x_ref, wap_ref, wlp_ref, wo_ref, g2_ref, rw_ref,
                 rb_ref, cin_ref, h1_ref, xn_ref, ei_ref, gt_ref, rk_ref, cnt_ref, carry):
    tm = x_ref.shape[0]

    @pl.when(pl.program_id(0) == 0)
    def _():
        carry[...] = cin_ref[...]

    attn_out = jnp.dot(o_ref[...], wap_ref[...], preferred_element_type=F32)
    lru_out = jnp.dot(hg_ref[...], wlp_ref[...], preferred_element_type=F32)
    merged = _sigmoid(ga_ref[...]) * attn_out + _sigmoid(gr_ref[...]) * lru_out
    h1 = x_ref[...] + jnp.dot(merged.astype(BF16), wo_ref[...], preferred_element_type=F32)
    h1_ref[...] = h1
    ms = jnp.mean(h1 * h1, axis=-1, keepdims=True)
    xn = (h1 * lax.rsqrt(ms + EPS)) * g2_ref[...]
    _store_row_tiles(xn_ref, xn)

    lane = lax.broadcasted_iota(I32, (tm, LANES), 1)
    logits = jnp.dot(xn.astype(BF16), rw_ref[...], preferred_element_type=F32) + rb_ref[...]
    lg = jnp.where(lane < N_EXPERTS, logits, NEG_BIG)
    tops, idxs, hots = [], [], []
    for _ in range(TOP_K):
        mx = jnp.max(lg, axis=-1, keepdims=True)
        idx = jnp.min(jnp.where(lg == mx, lane, LANES), axis=-1, keepdims=True)
        hot = lane == idx
        tops.append(mx); idxs.append(idx); hots.append(hot)
        lg = jnp.where(hot, NEG_BIG, lg)
    exps = [jnp.exp(tv - tops[0]) for tv in tops]
    den = exps[0] + exps[1] + exps[2] + exps[3]

    sel = jnp.zeros((tm, LANES), F32)
    for hot in hots:
        sel = sel + jnp.where(hot, 1.0, 0.0)
    r_i = lax.broadcasted_iota(I32, (tm, tm), 0)
    c_i = lax.broadcasted_iota(I32, (tm, tm), 1)
    ltri = jnp.where(c_i < r_i, 1.0, 0.0).astype(BF16)
    base = jnp.dot(ltri, sel.astype(BF16), preferred_element_type=F32) + carry[...]
    new_carry = carry[...] + jnp.sum(sel, axis=0, keepdims=True)
    carry[...] = new_carry
    cnt_ref[...] = new_carry

    ei = jnp.zeros((tm, LANES), I32)
    gt = jnp.zeros((tm, LANES), F32)
    rk = jnp.zeros((tm, LANES), I32)
    for k in range(TOP_K):
        rank = jnp.sum(jnp.where(hots[k], base, 0.0), axis=-1, keepdims=True)
        ei = jnp.where(lane == k, idxs[k], ei)
        gt = jnp.where(lane == k, exps[k] / den, gt)
        rk = jnp.where(lane == k, rank.astype(I32), rk)
    ei_ref[...] = ei
    gt_ref[...] = gt
    rk_ref[...] = rk


def _post(o, hg, ga, gr, x, p, cnt_in, tm):
    r, d = x.shape
    assert r % tm == 0
    row = lambda i: (i, 0)
    fix = lambda i: (0, 0)
    blk = pl.BlockSpec((tm, d), row)
    wsp = pl.BlockSpec((d, d), fix, pipeline_mode=pl.Buffered(1))
    lsp = pl.BlockSpec((tm, LANES), row)
    vec = pl.BlockSpec((1, LANES), fix)
    return pl.pallas_call(
        _post_kernel,
        grid=(r // tm,),
        in_specs=[blk, blk, blk, blk, blk, wsp, wsp, wsp, pl.BlockSpec((1, d), fix),
                  pl.BlockSpec((d, LANES), fix), vec, vec],
        out_specs=[blk, pl.BlockSpec((tm * SUBLANES, LANES), row), lsp, lsp, lsp, vec],
        out_shape=[jax.ShapeDtypeStruct((r, d), F32), jax.ShapeDtypeStruct((r * SUBLANES, LANES), F32),
                   jax.ShapeDtypeStruct((r, LANES), I32), jax.ShapeDtypeStruct((r, LANES), F32),
                   jax.ShapeDtypeStruct((r, LANES), I32), jax.ShapeDtypeStruct((1, LANES), F32)],
        scratch_shapes=[pltpu.VMEM((1, LANES), F32)],
        compiler_params=_cparams(("arbitrary",)),
        name="post_router",
    )(o, hg, ga, gr, x, p["w_attn_proj"], p["w_lru_proj"], p["w_out"], p["norm2_g"],
      p["router_w"], p["router_b"], cnt_in)


def _pad_rows_kernel(ps_ref, cnt_ref, rows_ref, zbuf, sem, *, tmx):
    zbuf[...] = jnp.zeros(zbuf.shape, F32)
    sizes = [1 << b for b in reversed(range((tmx - 1).bit_length()))]

    def visit(e, wait):
        cnt = cnt_ref[e]
        start = ps_ref[e] + cnt
        pad = lax.rem(tmx - lax.rem(cnt, tmx), tmx)
        for size in sizes:
            part = pad & size

            @pl.when(part != 0)
            def _():
                cp = pltpu.make_async_copy(zbuf.at[pl.ds(0, size)], rows_ref.at[pl.ds(start, size)], sem)
                cp.wait() if wait else cp.start()

            start = start + part

    def starts(e, c):
        visit(e, False)
        return c

    def waits(e, c):
        visit(e, True)
        return c

    lax.fori_loop(0, N_EXPERTS, starts, 0)
    lax.fori_loop(0, N_EXPERTS, waits, 0)


def _pad_rows(pstart, counts, n_rows, tmx):
    smem = pl.BlockSpec(memory_space=pltpu.SMEM)
    return pl.pallas_call(
        functools.partial(_pad_rows_kernel, tmx=tmx),
        in_specs=[smem, smem],
        out_specs=pl.BlockSpec(memory_space=pl.ANY),
        out_shape=jax.ShapeDtypeStruct((n_rows, SUBLANES, LANES), F32),
        scratch_shapes=[pltpu.VMEM((tmx // 2, SUBLANES, LANES), F32), pltpu.SemaphoreType.DMA(())],
        name="moe_pad_rows",
    )(pstart, counts)


def _dispatch_kernel(e_ref, rk_ref, ps_ref, xn_ref, rows_in_ref, rows_ref, sem, *, tm):
    del rows_in_ref

    def row_copy(j, d):
        return pltpu.make_async_copy(xn_ref.at[j], rows_ref.at[d], sem)

    def issue(jj, c):
        for u in range(DMA_UNROLL):
            j = jj * DMA_UNROLL + u
            for k in range(TOP_K):
                n = j * TOP_K + k
                row_copy(j, ps_ref[e_ref[n]] + rk_ref[n]).start()
        return c

    lax.fori_loop(0, tm // DMA_UNROLL, issue, 0)
    for _ in range(TOP_K):
        pltpu.make_async_copy(xn_ref, rows_ref.at[pl.ds(0, tm)], sem).wait()


def _dispatch(e_flat, rk_flat, pstart, xn, rows, tm):
    n = xn.shape[0]
    assert n % tm == 0
    smem = lambda shape, imap: pl.BlockSpec(shape, imap, memory_space=pltpu.SMEM)
    kern = functools.partial(_dispatch_kernel, tm=tm)
    return pl.pallas_call(
        kern,
        grid=(n // tm,),
        in_specs=[smem((tm * TOP_K,), lambda i: (i,)), smem((tm * TOP_K,), lambda i: (i,)),
                  smem(pstart.shape, lambda i: (0,)),
                  pl.BlockSpec((tm, SUBLANES, LANES), lambda i: (i, 0, 0)), pl.BlockSpec(memory_space=pl.ANY)],
        out_specs=pl.BlockSpec(memory_space=pl.ANY),
        out_shape=jax.ShapeDtypeStruct(rows.shape, rows.dtype),
        scratch_shapes=[pltpu.SemaphoreType.DMA(())],
        input_output_aliases={4: 0},
        compiler_params=_cparams(("arbitrary",)),
        name="moe_dispatch",
    )(e_flat, rk_flat, pstart, xn, rows)


def _expert_kernel(be_ref, nu_ref, x_ref, wg_ref, bg_ref, wu_ref, bu_ref, wd_ref, bd_ref, y_ref,
                   wg_s, wu_s, wd_s):
    j = pl.program_id(0)
    prev = be_ref[jnp.maximum(j - 1, 0)]

    @pl.when((j == 0) | (be_ref[j] != prev))
    def _():
        wg_s[...] = wg_ref[0].astype(BF16)
        wu_s[...] = wu_ref[0].astype(BF16)
        wd_s[...] = wd_ref[0].astype(BF16)

    @pl.when(j < nu_ref[0])
    def _():
        tm = x_ref.shape[0] // SUBLANES
        xb = _load_row_tiles(x_ref, tm).astype(BF16)
        g = jnp.dot(xb, wg_s[...], preferred_element_type=F32) + bg_ref[0]
        up = jnp.dot(xb, wu_s[...], preferred_element_type=F32) + bu_ref[0]
        g = jnp.minimum(g, SWIGLU_LIMIT)
        up = jnp.clip(up, -SWIGLU_LIMIT, SWIGLU_LIMIT)
        h = (up + 1.0) * (g * _sigmoid(SWIGLU_ALPHA * g))
        _store_row_tiles(y_ref, jnp.dot(h.astype(BF16), wd_s[...], preferred_element_type=F32) + bd_ref[0])


def _experts(blk_e, n_used, rows, p, tm):
    n_rows = rows.shape[0] // SUBLANES
    d = SUBLANES * LANES
    dff = p["w_gate"].shape[2]
    n_blk = n_rows // tm
    wsp = lambda shape: pl.BlockSpec((1,) + shape, lambda j, be, nu: (be[j], 0, 0))
    grid_spec = pltpu.PrefetchScalarGridSpec(
        num_scalar_prefetch=2,
        grid=(n_blk,),
        in_specs=[pl.BlockSpec((tm * SUBLANES, LANES), lambda j, be, nu: (j, 0)),
                  wsp((d, dff)), wsp((1, dff)), wsp((d, dff)), wsp((1, dff)),
                  wsp((dff, d)), wsp((1, d))],
        out_specs=pl.BlockSpec((tm * SUBLANES, LANES), lambda j, be, nu: (j, 0)),
        scratch_shapes=[pltpu.VMEM((d, dff), BF16), pltpu.VMEM((d, dff), BF16),
                        pltpu.VMEM((dff, d), BF16)],
    )
    return pl.pallas_call(
        _expert_kernel,
        grid_spec=grid_spec,
        out_shape=jax.ShapeDtypeStruct((n_rows * SUBLANES, LANES), F32),
        compiler_params=_cparams(("arbitrary",)),
        name="moe_experts",
    )(blk_e, n_used, rows, p["w_gate"], p["b_gate"], p["w_up"], p["b_up"], p["w_down"], p["b_down"])


def _combine_kernel(e_ref, rk_ref, en_ref, rkn_ref, ps_ref, y_ref, gt_ref, h1_ref, gf_ref, out_ref,
                    buf, sems, *, tm):
    i = pl.program_id(0)
    slot = i % 2

    def gather(er, rr, sl):
        def issue(jj, c):
            for u in range(DMA_UNROLL):
                j = jj * DMA_UNROLL + u
                for k in range(TOP_K):
                    n = j * TOP_K + k
                    dst = buf.at[sl, k, pl.ds(pl.multiple_of(j * SUBLANES, SUBLANES), SUBLANES), :]
                    pltpu.make_async_copy(y_ref.at[ps_ref[er[n]] + rr[n]], dst, sems.at[sl]).start()
            return c

        lax.fori_loop(0, tm // DMA_UNROLL, issue, 0)

    @pl.when(i == 0)
    def _():
        gather(e_ref, rk_ref, 0)

    @pl.when(i + 1 < pl.num_programs(0))
    def _():
        gather(en_ref, rkn_ref, 1 - slot)

    def finish(sl):
        for k in range(TOP_K):
            pltpu.make_async_copy(buf.at[sl, k], buf.at[sl, k], sems.at[sl]).wait()
        gt = gt_ref[...]
        y = _load_row_tiles(buf.at[sl, 0], tm) * gt[:, 0:1]
        for k in range(1, TOP_K):
            y = y + _load_row_tiles(buf.at[sl, k], tm) * gt[:, k:k + 1]
        h2 = h1_ref[...] + y
        ms = jnp.mean(h2 * h2, axis=-1, keepdims=True)
        out_ref[...] = (h2 * lax.rsqrt(ms + EPS)) * gf_ref[...]

    for sl in range(2):
        pl.when(slot == sl)(functools.partial(finish, sl))


def _combine(e_flat, rk_flat, pstart, y_rows, gt, h1, gf, tm):
    n, d = h1.shape
    assert n % tm == 0
    nt = n // tm
    smem = lambda shape, imap: pl.BlockSpec(shape, imap, memory_space=pltpu.SMEM)
    row = lambda i: (i, 0)
    cur = lambda i: (i,)
    nxt = lambda i: (jnp.minimum(i + 1, nt - 1),)
    kern = functools.partial(_combine_kernel, tm=tm)
    return pl.pallas_call(
        kern,
        grid=(nt,),
        in_specs=[smem((tm * TOP_K,), cur), smem((tm * TOP_K,), cur),
                  smem((tm * TOP_K,), nxt), smem((tm * TOP_K,), nxt),
                  smem(pstart.shape, lambda i: (0,)),
                  pl.BlockSpec(memory_space=pl.ANY),
                  pl.BlockSpec((tm, LANES), row), pl.BlockSpec((tm, d), row),
                  pl.BlockSpec((1, d), lambda i: (0, 0))],
        out_specs=pl.BlockSpec((tm, d), row),
        out_shape=jax.ShapeDtypeStruct((n, d), F32),
        scratch_shapes=[pltpu.VMEM((2, TOP_K, tm * SUBLANES, LANES), F32), pltpu.SemaphoreType.DMA((2,))],
        compiler_params=_cparams(("arbitrary",)),
        name="moe_combine",
    )(e_flat, rk_flat, e_flat, rk_flat, pstart, y_rows, gt, h1, gf)


def kernel(x_prompt, x_sample, cache_k, cache_v, state_conv, state_h, page_table, meta_tokens, norm1_g, w_in, lambda_q1, lambda_k1, lambda_q2, lambda_k2, subln_g, conv_w, conv_b, rg_wa, rg_ba, rg_wx, rg_bx, rg_lambda, w_attn_proj, w_lru_proj, w_out, norm2_g, router_w, router_b, w_gate, b_gate, w_up, b_up, w_down, b_down, final_norm_g):
    depth = w_in.shape[0]
    assert depth == 1, "single-layer stack"
    b, t, d = x_prompt.shape
    bd, s_new, _ = x_sample.shape
    n_p, n_s = b * t, bd * s_new
    row1 = lambda a: a.reshape(1, -1)

    w_in_bf = w_in[0].astype(BF16)
    g1 = row1(norm1_g[0])
    lv = jnp.stack([lambda_q1[0], lambda_k1[0], lambda_q2[0], lambda_k2[0]])
    sg = row1(subln_g[0])
    lru_p = dict(conv_w=conv_w[0], conv_b=row1(conv_b[0]), wa=rg_wa[0].astype(BF16), ba=row1(rg_ba[0]),
                 wx=rg_wx[0].astype(BF16), bx=row1(rg_bx[0]), lam=row1(rg_lambda[0]))
    rw = jnp.zeros((d, LANES), BF16).at[:, :N_EXPERTS].set(router_w[0].astype(BF16))
    rb = jnp.zeros((1, LANES), F32).at[0, :N_EXPERTS].set(router_b[0])
    post_p = dict(w_attn_proj=w_attn_proj[0].astype(BF16), w_lru_proj=w_lru_proj[0].astype(BF16),
                  w_out=w_out[0].astype(BF16), norm2_g=row1(norm2_g[0]), router_w=rw, router_b=rb)
    exp_p = dict(w_gate=w_gate[0], b_gate=b_gate[0][:, None, :], w_up=w_up[0], b_up=b_up[0][:, None, :],
                 w_down=w_down[0], b_down=b_down[0][:, None, :])

    xp = x_prompt.reshape(n_p, d)
    xs = x_sample.reshape(n_s, d)
    qbP, kbP, vbP, k3P, v3P, xrP, yrP, gaP, grP = _inproj(xp, g1, w_in_bf, _tile(t, ROW_TILE), (t, N_META))
    qbS, kbS, vbS, k3S, v3S, xrS, yrS, gaS, grS = _inproj(xs, g1, w_in_bf, _tile(n_s, ROW_TILE))
    _, kbM, vbM, k3M, v3M, xrM, yrM, _, _ = _inproj(meta_tokens, g1, w_in_bf, N_META)

    oP = _attn_prompt(lv, qbP.reshape(b, t, d), kbP.reshape(b, t, d), vbP.reshape(b, t, d), kbM, vbM, sg)
    n_pool, page = cache_k.shape[1], cache_k.shape[2]
    oS = _attn_sample(page_table, lv, qbS.astype(F32).reshape(bd, s_new, d),
                      k3S.reshape(bd, s_new * N_HEADS, HEAD_W), v3S.reshape(bd, s_new * N_HEADS, HEAD_W), sg,
                      cache_k[0].reshape(n_pool, page * N_HEADS, HEAD_W),
                      cache_v[0].reshape(n_pool, page * N_HEADS, HEAD_W))

    gb = LRU_GB
    zc = jnp.zeros((gb, CONV_W - 1, d), F32)
    zh = jnp.zeros((gb, d), F32)
    bc = lambda a: jnp.broadcast_to(a[None], (gb,) + a.shape)
    _, hM = _lru(bc(xrM), bc(yrM), zc, zh, lru_p, N_META, N_META)
    cbP = jnp.broadcast_to(xrM[None, N_META - (CONV_W - 1):], (b, CONV_W - 1, d))
    h0P = jnp.broadcast_to(hM[0:1], (b, d))
    xrP3 = xrP.reshape(b, t, d)
    hgP, hlP = _lru(xrP3, yrP.reshape(b, t, d), cbP, h0P, lru_p, _tile(t, LRU_TT), _tile(t, LRU_TT))
    t_pad = SUBLANES
    padt = lambda a: jnp.pad(a.reshape(bd, s_new, d), ((0, 0), (0, t_pad - s_new), (0, 0)))
    xrS3 = xrS.reshape(bd, s_new, d)
    hgS, hlS = _lru(padt(xrS), padt(yrS), state_conv[0], state_h[0], lru_p, t_pad, s_new)
    hgS = hgS[:, :s_new].reshape(n_s, d)

    cnt0 = jnp.zeros((1, LANES), F32)
    h1P, xnP, eiP, gtP, rkP, cntP = _post(oP.reshape(n_p, d), hgP.reshape(n_p, d), gaP, grP, xp, post_p,
                                          cnt0, _tile(n_p, ROW_TILE))
    h1S, xnS, eiS, gtS, rkS, cnt = _post(oS.reshape(n_s, d).astype(BF16), hgS, gaS, grS, xs, post_p,
                                         cntP, _tile(n_s, ROW_TILE))

    n_tok = n_p + n_s
    tmx = MOE_TILE
    counts = cnt[0, :N_EXPERTS].astype(I32)
    padded = (counts + tmx - 1) // tmx * tmx
    pend = jnp.cumsum(padded)
    pstart = (pend - padded).astype(I32)
    n_rows = -(-(n_tok * TOP_K + N_EXPERTS * (tmx - 1)) // tmx) * tmx
    n_blk = n_rows // tmx
    blk_start = jnp.arange(n_blk, dtype=I32) * tmx
    blk_e = jnp.minimum(jnp.sum((pend[None, :] <= blk_start[:, None]).astype(I32), axis=1), N_EXPERTS - 1)
    n_used = (pend[-1:] // tmx).astype(I32)
    flat = lambda a: a[:, :TOP_K].reshape(-1)

    tiles = lambda a: a.reshape(-1, SUBLANES, LANES)
    rows = _pad_rows(pstart, counts, n_rows, tmx)
    rows = _dispatch(flat(eiP), flat(rkP), pstart, tiles(xnP), rows, _tile(n_p, DISPATCH_TILE))
    rows = _dispatch(flat(eiS), flat(rkS), pstart, tiles(xnS), rows, _tile(n_s, DISPATCH_TILE))
    y_rows = tiles(_experts(blk_e, n_used, rows.reshape(-1, LANES), exp_p, tmx))
    gf = row1(final_norm_g)
    yP = _combine(flat(eiP), flat(rkP), pstart, y_rows, gtP, h1P, gf, _tile(n_p, COMBINE_TILE))
    yS = _combine(flat(eiS), flat(rkS), pstart, y_rows, gtS, h1S, gf, _tile(n_s, COMBINE_TILE))

    def with_meta(x3, m3):
        return _fill_prefix(m3, x3, b).reshape(1, b, t + N_META, N_HEADS, HEAD_W)

    nc = CONV_W - 1
    return (yP.reshape(b, t, d), yS.reshape(bd, s_new, d),
            with_meta(k3P, k3M), with_meta(v3P, v3M),
            xrP3[:, t - nc:][None], hlP[None],
            k3S.reshape(1, bd, s_new, N_HEADS, HEAD_W), v3S.reshape(1, bd, s_new, N_HEADS, HEAD_W),
            xrS3[:, s_new - nc:][None], hlS[None])
```

```python
import functools
import math

import jax
import jax.numpy as jnp
from jax import lax
from jax.experimental import pallas as pl
from jax.experimental.pallas import tpu as pltpu

F32 = jnp.float32
BF16 = jnp.bfloat16
I32 = jnp.int32

LANES = 128
SUBLANES = 8
N_HEADS = 8
HEAD_W = 128
MAP_W = 64
N_META = 16
CONV_W = 4
LRU_BLOCKS = 4
LRU_C = 8.0
N_EXPERTS = 32
TOP_K = 4
SWIGLU_LIMIT = 7.0
SWIGLU_ALPHA = 1.702
EPS = 1e-6
SUBLN_EPS = 1e-5
LAM_INIT = 0.8 - 0.6 * math.exp(-0.3 * 0)
NEG_BIG = -1e30
VMEM_LIMIT = 56 * 1024 * 1024

ROW_TILE = 256
MOE_TILE = 512
ATT_TILE = 512
ATT_HEADS = 2
ATT_STRIP = 32
LRU_TT = 64
LRU_GB = 8
PAGES_PER_STEP = 16
SOFTMAX_SETS = 4
SOFTMAX_STRIP = 16
DISPATCH_TILE = 512
COMBINE_TILE = 256
DMA_UNROLL = 4


def _cparams(sem):
    return pltpu.CompilerParams(dimension_semantics=sem, vmem_limit_bytes=VMEM_LIMIT)


def _tile(n, pref):
    t = min(pref, n)
    while n % t:
        t //= 2
    return t


def _store_row_tiles(ref, val):
    n = val.shape[0]
    for s in range(SUBLANES):
        ref[pl.ds(s, n, stride=SUBLANES), :] = val[:, s * LANES:(s + 1) * LANES]


def _load_row_tiles(ref, n):
    return jnp.concatenate([ref[pl.ds(s, n, stride=SUBLANES), :] for s in range(SUBLANES)], axis=-1)


def _sigmoid(x):
    return 1.0 / (1.0 + jnp.exp(-x))


def _expm1(x):
    u = jnp.exp(x)
    um1 = u - 1.0
    return jnp.where(u == 1.0, x, jnp.where(um1 == -1.0, -1.0, um1 * x / jnp.log(u)))


def _lam_from(lv):
    a = jnp.sum(lv[0:1, :] * lv[1:2, :], axis=-1, keepdims=True)
    b = jnp.sum(lv[2:3, :] * lv[3:4, :], axis=-1, keepdims=True)
    return jnp.exp(a) - jnp.exp(b) + LAM_INIT


def _subln(o, g):
    ms = jnp.mean(o * o, axis=-1, keepdims=True)
    return ((o * lax.rsqrt(ms + SUBLN_EPS)) * g) * (1.0 - LAM_INIT)


def _inproj_kernel(x_ref, g_ref, w_ref, qb_ref, kb_ref, vb_ref, k3_ref, v3_ref,
                   xr_ref, yr_ref, ga_ref, gr_ref):
    tm, d = x_ref.shape
    x = x_ref[...]
    ms = jnp.mean(x * x, axis=-1, keepdims=True)
    ub = ((x * lax.rsqrt(ms + EPS)) * g_ref[...]).astype(BF16)

    def proj(j):
        return jnp.dot(ub, w_ref[:, j * d:(j + 1) * d], preferred_element_type=F32)

    q = proj(0)
    qb_ref[...] = (q * (MAP_W ** -0.5)).astype(BF16)
    k = proj(1)
    kb_ref[...] = k.astype(BF16)
    _store_row_tiles(k3_ref, k)
    v = proj(2)
    vb_ref[...] = v.astype(BF16)
    _store_row_tiles(v3_ref, v)
    xr_ref[...] = proj(3)
    yr_ref[...] = proj(4)
    ga_ref[...] = proj(5)
    gr_ref[...] = proj(6)


def _inproj(x, g, w_bf, tm, seq_prefix=None):
    r, d = x.shape
    assert r % tm == 0
    row = lambda i: (i, 0)
    f32o = jax.ShapeDtypeStruct((r, d), F32)
    bfo = jax.ShapeDtypeStruct((r, d), BF16)
    blk = pl.BlockSpec((tm, d), row)
    if seq_prefix is None:
        h3o = jax.ShapeDtypeStruct((r * N_HEADS, HEAD_W), F32)
        blk3 = pl.BlockSpec((tm * N_HEADS, HEAD_W), row)
    else:
        t, n_pre = seq_prefix
        assert t % tm == 0
        tiles = t // tm
        h3o = jax.ShapeDtypeStruct((r // t * (t + n_pre) * N_HEADS, HEAD_W), F32)
        blk3 = pl.BlockSpec(
            (pl.Element(tm * N_HEADS), pl.Element(HEAD_W)),
            lambda i: (((i // tiles) * (t + n_pre) + n_pre + (i % tiles) * tm) * N_HEADS, 0))
    return pl.pallas_call(
        _inproj_kernel,
        grid=(r // tm,),
        in_specs=[blk, pl.BlockSpec((1, d), lambda i: (0, 0)),
                  pl.BlockSpec(w_bf.shape, lambda i: (0, 0), pipeline_mode=pl.Buffered(1))],
        out_specs=[blk, blk, blk, blk3, blk3, blk, blk, blk, blk],
        out_shape=[bfo, bfo, bfo, h3o, h3o, f32o, f32o, f32o, f32o],
        compiler_params=_cparams(("parallel",)),
        name="inproj",
    )(x, g, w_bf)


def _fill_prefix_kernel(m_ref, buf_ref, o_ref):
    del buf_ref
    o_ref[...] = m_ref[...]


def _fill_prefix(m3, buf, n_seq):
    pre = m3.shape[0]
    stride = buf.shape[0] // n_seq
    assert stride % pre == 0
    return pl.pallas_call(
        _fill_prefix_kernel,
        grid=(n_seq,),
        in_specs=[pl.BlockSpec(m3.shape, lambda s: (0, 0)), pl.BlockSpec(memory_space=pl.ANY)],
        out_specs=pl.BlockSpec(m3.shape, lambda s: (s * (stride // pre), 0)),
        out_shape=jax.ShapeDtypeStruct(buf.shape, buf.dtype),
        input_output_aliases={1: 0},
        compiler_params=_cparams(("arbitrary",)),
        name="fill_prefix",
    )(m3, buf)


def _attn_prompt_kernel(lv_ref, q_ref, k_ref, v_ref, km_ref, vm_ref, g_ref, o_ref,
                        qq_scr, s_scr, p_scr, m_scr, l_scr, a_scr, acc_scr, *, tq, hp, rb):
    qi = pl.program_id(2)
    rows = 2 * tq
    nc = tq // LANES
    lam = _lam_from(lv_ref[...])
    lane = lax.broadcasted_iota(I32, (tq, HEAD_W), 1)
    heads = [slice(h * HEAD_W, (h + 1) * HEAD_W) for h in range(hp)]
    wide = lambda x: jnp.broadcast_to(x, (x.shape[0], LANES))

    def scores(qq, kc):
        return lax.dot_general(qq, kc, (((1,), (1,)), ((), ())), preferred_element_type=F32)

    for h, sl in enumerate(heads):
        q = q_ref[0, :, sl]
        zero = jnp.zeros_like(q)
        qq_scr[h, :tq, :] = jnp.where(lane < MAP_W, q, zero)
        qq_scr[h, tq:, :] = jnp.where(lane >= MAP_W, q, zero)
        s0 = scores(qq_scr[h], km_ref[:, sl])
        m = jnp.max(s0, axis=-1, keepdims=True)
        p0 = jnp.exp(s0 - m)
        m_scr[h] = wide(m)
        l_scr[h] = wide(jnp.sum(p0, axis=-1, keepdims=True))
        acc_scr[h] = jnp.dot(p0.astype(BF16), vm_ref[:, sl], preferred_element_type=F32)

    def key_tile(h, sl, off, causal):
        s_scr[h] = scores(qq_scr[h], k_ref[0, pl.ds(off, tq), sl])
        for r in range(rows // rb):
            rs = slice(r * rb, (r + 1) * rb)
            q0 = (r * rb) % tq
            live = min(nc, (q0 + rb - 1) // LANES + 1) if causal else nc
            cols = [s_scr[h, rs, c * LANES:(c + 1) * LANES] for c in range(live)]
            for c in range(live, nc):
                p_scr[h, rs, c * LANES:(c + 1) * LANES] = jnp.zeros((rb, LANES), BF16)
            if causal:
                qpos = lax.broadcasted_iota(I32, (rb, LANES), 0) + q0
                kpos = lax.broadcasted_iota(I32, (rb, LANES), 1)
                cols = [jnp.where(kpos + c * LANES <= qpos, s, -jnp.inf) for c, s in enumerate(cols)]
            mx = cols[0]
            for s in cols[1:]:
                mx = jnp.maximum(mx, s)
            m_prev = m_scr[h, rs, :]
            m_new = jnp.maximum(m_prev, jnp.max(mx, axis=-1, keepdims=True))
            alpha = jnp.exp(m_prev - m_new)
            psum = None
            for c, s in enumerate(cols):
                p = jnp.exp(s - m_new)
                psum = p if psum is None else psum + p
                p_scr[h, rs, c * LANES:(c + 1) * LANES] = p.astype(BF16)
            l_scr[h, rs, :] = alpha * l_scr[h, rs, :] + jnp.sum(psum, axis=-1, keepdims=True)
            m_scr[h, rs, :] = m_new
            a_scr[h, rs, :] = alpha
        acc_scr[h] = a_scr[h] * acc_scr[h] + jnp.dot(p_scr[h], v_ref[0, pl.ds(off, tq), sl],
                                                       preferred_element_type=F32)

    def body(j, c):
        off = pl.multiple_of(j * tq, tq)
        for h, sl in enumerate(heads):
            key_tile(h, sl, off, False)
        return c

    lax.fori_loop(0, qi, body, 0)

    off = pl.multiple_of(qi * tq, tq)
    for h, sl in enumerate(heads):
        key_tile(h, sl, off, True)
        o = acc_scr[h] / l_scr[h]
        o = o[:tq] - lam * o[tq:]
        o_ref[0, :, sl] = _subln(o, g_ref[...]).astype(o_ref.dtype)


def _attn_prompt(lv, qb, kb, vb, kmb, vmb, subln_g):
    b, t, d = qb.shape
    tq = _tile(t, ATT_TILE)
    hp = ATT_HEADS
    w = hp * HEAD_W
    assert tq % LANES == 0 and tq % ATT_STRIP == 0
    rows = 2 * tq
    kern = functools.partial(_attn_prompt_kernel, tq=tq, hp=hp, rb=ATT_STRIP)
    stat = pltpu.VMEM((hp, rows, LANES), F32)
    return pl.pallas_call(
        kern,
        grid=(b, N_HEADS // hp, t // tq),
        scratch_shapes=[pltpu.VMEM((hp, rows, HEAD_W), BF16), pltpu.VMEM((hp, rows, tq), F32),
                        pltpu.VMEM((hp, rows, tq), BF16), stat, stat, stat, stat],
        in_specs=[pl.BlockSpec(lv.shape, lambda bi, h, i: (0, 0)),
                  pl.BlockSpec((1, tq, w), lambda bi, h, i: (bi, i, h)),
                  pl.BlockSpec((1, t, w), lambda bi, h, i: (bi, 0, h)),
                  pl.BlockSpec((1, t, w), lambda bi, h, i: (bi, 0, h)),
                  pl.BlockSpec((N_META, w), lambda bi, h, i: (0, h)),
                  pl.BlockSpec((N_META, w), lambda bi, h, i: (0, h)),
                  pl.BlockSpec((1, HEAD_W), lambda bi, h, i: (0, 0))],
        out_specs=pl.BlockSpec((1, tq, w), lambda bi, h, i: (bi, i, h)),
        out_shape=jax.ShapeDtypeStruct((b, t, d), BF16),
        compiler_params=_cparams(("parallel", "parallel", "arbitrary")),
        name="attn_prompt",
    )(lv, qb, kb, vb, kmb, vmb, subln_g)


def _attn_sample_kernel(pt_ref, lv_ref, q_ref, kn_ref, vn_ref, g_ref, *rest, pp, s_new):
    kpages = rest[:pp]
    vpages = rest[pp:2 * pp]
    o_ref = rest[2 * pp]
    qq_scr, s_scr, p_scr, m_scr, l_scr, a_scr, acc_scr = rest[2 * pp + 1:]
    g = pl.program_id(1)
    ng = pl.num_programs(1)
    hr = 2 * s_new
    rows = N_HEADS * hr

    def head_bias(shape):
        row = lax.broadcasted_iota(I32, shape, 0)
        col = lax.broadcasted_iota(I32, shape, 1)
        return row, col, (col % N_HEADS) == (row // hr)

    @pl.when(g == 0)
    def _():
        lane = lax.broadcasted_iota(I32, (s_new, HEAD_W), 1)
        parts = []
        for h in range(N_HEADS):
            qh = q_ref[0, :, h * HEAD_W:(h + 1) * HEAD_W]
            parts += [jnp.where(lane < MAP_W, qh, 0.0), jnp.where(lane >= MAP_W, qh, 0.0)]
        qq_scr[...] = jnp.concatenate(parts, axis=0).astype(BF16)
        m_scr[...] = jnp.full(m_scr.shape, -jnp.inf, F32)
        l_scr[...] = jnp.zeros(l_scr.shape, F32)
        acc_scr[...] = jnp.zeros(acc_scr.shape, F32)

    def scores(kc):
        return lax.dot_general(qq_scr[...], kc, (((1,), (1,)), ((), ())), preferred_element_type=F32)

    def update(a, s_list, v_list):
        m_prev = m_scr[a]
        s_max = s_list[0]
        for s in s_list[1:]:
            s_max = jnp.maximum(s_max, s)
        m_new = jnp.maximum(m_prev, jnp.max(s_max, axis=-1, keepdims=True))
        alpha = jnp.exp(m_prev - m_new)
        l_add = None
        pv = None
        for s, vc in zip(s_list, v_list):
            p = jnp.exp(s - m_new[:, :1])
            ps = jnp.sum(p, axis=-1, keepdims=True)
            l_add = ps if l_add is None else l_add + ps
            d = jnp.dot(p.astype(BF16), vc, preferred_element_type=F32)
            pv = d if pv is None else pv + d
        m_scr[a] = m_new
        l_scr[a] = alpha * l_scr[a] + l_add
        acc_scr[a] = alpha * acc_scr[a] + pv

    def update_pages(a, pages):
        for i in pages:
            s_scr[i] = scores(kpages[i][...].astype(BF16))
        blocks = [(i, c) for i in pages for c in range(s_scr.shape[2] // LANES)]
        for r0 in range(0, rows, SOFTMAX_STRIP):
            rs = slice(r0, r0 + SOFTMAX_STRIP)
            head = (lax.broadcasted_iota(I32, (SOFTMAX_STRIP, LANES), 0) + r0) // hr
            lane = lax.broadcasted_iota(I32, (SOFTMAX_STRIP, LANES), 1)
            bias = jnp.where(lane % N_HEADS == head, 0.0, -jnp.inf)
            mx = None
            for i, c in blocks:
                s = s_scr[i, rs, c * LANES:(c + 1) * LANES] + bias
                mx = s if mx is None else jnp.maximum(mx, s)
            m_prev = m_scr[a, rs, :]
            m_new = jnp.maximum(m_prev, jnp.max(mx, axis=-1, keepdims=True))
            alpha = jnp.exp(m_prev - m_new)
            psum = None
            for i, c in blocks:
                p = jnp.exp(s_scr[i, rs, c * LANES:(c + 1) * LANES] + bias - m_new)
                psum = p if psum is None else psum + p
                p_scr[i, rs, c * LANES:(c + 1) * LANES] = p.astype(BF16)
            m_scr[a, rs, :] = m_new
            l_scr[a, rs, :] = alpha * l_scr[a, rs, :] + jnp.sum(psum, axis=-1, keepdims=True)
            a_scr[a, rs, :] = alpha
        pv = None
        for i in pages:
            d = jnp.dot(p_scr[i], vpages[i][...].astype(BF16), preferred_element_type=F32)
            pv = d if pv is None else pv + d
        acc_scr[a] = a_scr[a] * acc_scr[a] + pv

    n_acc = m_scr.shape[0]
    per = pp // n_acc
    for a in range(n_acc):
        update_pages(a, range(a * per, (a + 1) * per))

    @pl.when(g == ng - 1)
    def _():
        lam = _lam_from(lv_ref[...])
        row, col, same = head_bias((rows, s_new * N_HEADS))
        causal = (col // N_HEADS) <= ((row % hr) % s_new)
        s = jnp.where(same & causal, scores(kn_ref[0].astype(BF16)), -jnp.inf)
        update(0, [s], [vn_ref[0].astype(BF16)])
        m = m_scr[0]
        for a in range(1, n_acc):
            m = jnp.maximum(m, m_scr[a])
        l = jnp.zeros_like(m)
        acc = jnp.zeros(acc_scr.shape[1:], F32)
        for a in range(n_acc):
            w = jnp.exp(m_scr[a] - m)
            l = l + w * l_scr[a]
            acc = acc + w * acc_scr[a]
        o = acc / l
        for h in range(N_HEADS):
            oh = o[h * hr:h * hr + s_new] - lam * o[h * hr + s_new:(h + 1) * hr]
            o_ref[0, :, h * HEAD_W:(h + 1) * HEAD_W] = _subln(oh, g_ref[...])


def _attn_sample(page_table, lv, q, kn, vn, subln_g, cache_k, cache_v):
    bd, s_new, d = q.shape
    n_pages = page_table.shape[1]
    page_rows = cache_k.shape[1]
    pp = _tile(n_pages, PAGES_PER_STEP)
    n_acc = _tile(pp, SOFTMAX_SETS)
    rows = N_HEADS * 2 * s_new
    pt = page_table.reshape(-1)
    seq = lambda b, g, pt: (b, 0, 0)

    def page_spec(p):
        return pl.BlockSpec((None, page_rows, HEAD_W),
                            lambda b, g, pt, p=p: (pt[b * n_pages + g * pp + p], 0, 0))

    grid_spec = pltpu.PrefetchScalarGridSpec(
        num_scalar_prefetch=1,
        grid=(bd, n_pages // pp),
        in_specs=[pl.BlockSpec(lv.shape, lambda b, g, pt: (0, 0)),
                  pl.BlockSpec((1, s_new, d), seq),
                  pl.BlockSpec((1, s_new * N_HEADS, HEAD_W), seq),
                  pl.BlockSpec((1, s_new * N_HEADS, HEAD_W), seq),
                  pl.BlockSpec((1, HEAD_W), lambda b, g, pt: (0, 0))]
                 + [page_spec(p) for p in range(pp)] + [page_spec(p) for p in range(pp)],
        out_specs=pl.BlockSpec((1, s_new, d), seq),
        scratch_shapes=[pltpu.VMEM((rows, HEAD_W), BF16),
                        pltpu.VMEM((pp, rows, page_rows), F32),
                        pltpu.VMEM((pp, rows, page_rows), BF16),
                        pltpu.VMEM((n_acc, rows, LANES), F32),
                        pltpu.VMEM((n_acc, rows, LANES), F32),
                        pltpu.VMEM((n_acc, rows, LANES), F32),
                        pltpu.VMEM((n_acc, rows, HEAD_W), F32)],
    )
    kern = functools.partial(_attn_sample_kernel, pp=pp, s_new=s_new)
    return pl.pallas_call(
        kern,
        grid_spec=grid_spec,
        out_shape=jax.ShapeDtypeStruct((bd, s_new, d), F32),
        compiler_params=_cparams(("parallel", "arbitrary")),
        name="attn_sample",
    )(pt, lv, q, kn, vn, subln_g, *([cache_k] * pp), *([cache_v] * pp))


def _lru_kernel(xr_ref, yr_ref, cb0_ref, h0_ref, cw_ref, cb_ref, wa_ref, ba_ref, wx_ref, bx_ref,
                lam_ref, hg_ref, hl_ref, xbuf, a_scr, u_scr, h_scr, *, n_steps):
    gb, tt, w = xr_ref.shape
    bw = w // LRU_BLOCKS
    i = pl.program_id(1)
    halo = SUBLANES

    @pl.when(i == 0)
    def _():
        xbuf[:, halo - (CONV_W - 1):halo, :] = cb0_ref[...]
        h_scr[...] = h0_ref[...]

    xbuf[:, halo:halo + tt, :] = xr_ref[...]
    xc = cb_ref[...] + xbuf[:, halo - 3:halo - 3 + tt, :] * cw_ref[0:1, :]
    for j in range(1, CONV_W):
        xc = xc + xbuf[:, halo - 3 + j:halo - 3 + j + tt, :] * cw_ref[j:j + 1, :]
    xbuf[:, halo - (CONV_W - 1):halo, :] = xbuf[:, halo + tt - (CONV_W - 1):halo + tt, :]

    xc2 = xc.reshape(gb * tt, w)
    xcb = xc2.astype(BF16)
    r_parts, i_parts = [], []
    for c in range(LRU_BLOCKS):
        xs = xcb[:, c * bw:(c + 1) * bw]
        r_parts.append(jnp.dot(xs, wa_ref[c], preferred_element_type=F32))
        i_parts.append(jnp.dot(xs, wx_ref[c], preferred_element_type=F32))
    r = _sigmoid(jnp.concatenate(r_parts, axis=-1) + ba_ref[...])
    ig = _sigmoid(jnp.concatenate(i_parts, axis=-1) + bx_ref[...])
    nl = -lam_ref[...]
    softplus = jnp.maximum(nl, 0.0) + jnp.log1p(jnp.exp(-jnp.abs(nl)))
    log_a = (-LRU_C * r) * softplus
    a_scr[...] = jnp.exp(log_a).reshape(gb, tt, w)
    mult = jnp.sqrt(-_expm1(2.0 * log_a))
    u_scr[...] = (xc2 * ig * mult).reshape(gb, tt, w)

    def step(t, h):
        h = a_scr[:, t, :] * h + u_scr[:, t, :]
        u_scr[:, t, :] = h
        return h

    h = lax.fori_loop(0, n_steps, step, h_scr[...])
    h_scr[...] = h
    hl_ref[...] = h
    y = yr_ref[...]
    gelu = y * (0.5 * (1.0 + jnp.tanh(math.sqrt(2.0 / math.pi) * (y + 0.044715 * (y * y * y)))))
    hg_ref[...] = (u_scr[...] * gelu).astype(hg_ref.dtype)


def _lru(xr, yr, cb0, h0, p, tt, n_steps):
    b, t, w = xr.shape
    gb = LRU_GB
    assert b % gb == 0 and t % tt == 0 and (n_steps == tt or t == tt)
    bw = w // LRU_BLOCKS
    blk = pl.BlockSpec((gb, tt, w), lambda g, i: (g, i, 0))
    vec = pl.BlockSpec((1, w), lambda g, i: (0, 0))
    wsp = pl.BlockSpec((LRU_BLOCKS, bw, bw), lambda g, i: (0, 0, 0))
    kern = functools.partial(_lru_kernel, n_steps=n_steps)
    return pl.pallas_call(
        kern,
        grid=(b // gb, t // tt),
        in_specs=[blk, blk,
                  pl.BlockSpec((gb, CONV_W - 1, w), lambda g, i: (g, 0, 0)),
                  pl.BlockSpec((gb, w), lambda g, i: (g, 0)),
                  pl.BlockSpec((CONV_W, w), lambda g, i: (0, 0)), vec, wsp, vec, wsp, vec, vec],
        out_specs=[blk, pl.BlockSpec((gb, w), lambda g, i: (g, 0))],
        out_shape=[jax.ShapeDtypeStruct((b, t, w), BF16), jax.ShapeDtypeStruct((b, w), F32)],
        scratch_shapes=[pltpu.VMEM((gb, tt + SUBLANES, w), F32), pltpu.VMEM((gb, tt, w), F32),
                        pltpu.VMEM((gb, tt, w), F32), pltpu.VMEM((gb, w), F32)],
        compiler_params=_cparams(("parallel", "arbitrary")),
        name="lru",
    )(xr, yr, cb0, h0, p["conv_w"], p["conv_b"], p["wa"], p["ba"], p["wx"], p["bx"], p["lam"])


def _post_kernel(o_ref, hg_ref, ga_ref, gr_ref, x_ref, wap_ref, wlp_ref, wo_ref, g2_ref, rw_ref,
                 rb_ref, cin_ref, h1_ref, xn_ref, ei_ref, gt_ref, rk_ref, cnt_ref, carry):
    tm = x_ref.shape[0]

    @pl.when(pl.program_id(0) == 0)
    def _():
        carry[...] = cin_ref[...]

    attn_out = jnp.dot(o_ref[...], wap_ref[...], preferred_element_type=F32)
    lru_out = jnp.dot(hg_ref[...], wlp_ref[...], preferred_element_type=F32)
    merged = _sigmoid(ga_ref[...]) * attn_out + _sigmoid(gr_ref[...]) * lru_out
    h1 = x_ref[...] + jnp.dot(merged.astype(BF16), wo_ref[...], preferred_element_type=F32)
    h1_ref[...] = h1
    ms = jnp.mean(h1 * h1, axis=-1, keepdims=True)
    xn = (h1 * lax.rsqrt(ms + EPS)) * g2_ref[...]
    _store_row_tiles(xn_ref, xn)

    lane = lax.broadcasted_iota(I32, (tm, LANES), 1)
    logits = jnp.dot(xn.astype(BF16), rw_ref[...], preferred_element_type=F32) + rb_ref[...]
    lg = jnp.where(lane < N_EXPERTS, logits, NEG_BIG)
    tops, idxs, hots = [], [], []
    for _ in range(TOP_K):
        mx = jnp.max(lg, axis=-1, keepdims=True)
        idx = jnp.min(jnp.where(lg == mx, lane, LANES), axis=-1, keepdims=True)
        hot = lane == idx
        tops.append(mx); idxs.append(idx); hots.append(hot)
        lg = jnp.where(hot, NEG_BIG, lg)
    exps = [jnp.exp(tv - tops[0]) for tv in tops]
    den = exps[0] + exps[1] + exps[2] + exps[3]

    sel = jnp.zeros((tm, LANES), F32)
    for hot in hots:
        sel = sel + jnp.where(hot, 1.0, 0.0)
    r_i = lax.broadcasted_iota(I32, (tm, tm), 0)
    c_i = lax.broadcasted_iota(I32, (tm, tm), 1)
    ltri = jnp.where(c_i < r_i, 1.0, 0.0).astype(BF16)
    base = jnp.dot(ltri, sel.astype(BF16), preferred_element_type=F32) + carry[...]
    new_carry = carry[...] + jnp.sum(sel, axis=0, keepdims=True)
    carry[...] = new_carry
    cnt_ref[...] = new_carry

    ei = jnp.zeros((tm, LANES), I32)
    gt = jnp.zeros((tm, LANES), F32)
    rk = jnp.zeros((tm, LANES), I32)
    for k in range(TOP_K):
        rank = jnp.sum(jnp.where(hots[k], base, 0.0), axis=-1, keepdims=True)
        ei = jnp.where(lane == k, idxs[k], ei)
        gt = jnp.where(lane == k, exps[k] / den, gt)
        rk = jnp.where(lane == k, rank.astype(I32), rk)
    ei_ref[...] = ei
    gt_ref[...] = gt
    rk_ref[...] = rk


def _post(o, hg, ga, gr, x, p, cnt_in, tm):
    r, d = x.shape
    assert r % tm == 0
    row = lambda i: (i, 0)
    fix = lambda i: (0, 0)
    blk = pl.BlockSpec((tm, d), row)
    wsp = pl.BlockSpec((d, d), fix, pipeline_mode=pl.Buffered(1))
    lsp = pl.BlockSpec((tm, LANES), row)
    vec = pl.BlockSpec((1, LANES), fix)
    return pl.pallas_call(
        _post_kernel,
        grid=(r // tm,),
        in_specs=[blk, blk, blk, blk, blk, wsp, wsp, wsp, pl.BlockSpec((1, d), fix),
                  pl.BlockSpec((d, LANES), fix), vec, vec],
        out_specs=[blk, pl.BlockSpec((tm * SUBLANES, LANES), row), lsp, lsp, lsp, vec],
        out_shape=[jax.ShapeDtypeStruct((r, d), F32), jax.ShapeDtypeStruct((r * SUBLANES, LANES), F32),
                   jax.ShapeDtypeStruct((r, LANES), I32), jax.ShapeDtypeStruct((r, LANES), F32),
                   jax.ShapeDtypeStruct((r, LANES), I32), jax.ShapeDtypeStruct((1, LANES), F32)],
        scratch_shapes=[pltpu.VMEM((1, LANES), F32)],
        compiler_params=_cparams(("arbitrary",)),
        name="post_router",
    )(o, hg, ga, gr, x, p["w_attn_proj"], p["w_lru_proj"], p["w_out"], p["norm2_g"],
      p["router_w"], p["router_b"], cnt_in)


def _pad_rows_kernel(ps_ref, cnt_ref, rows_ref, zbuf, sem, *, tmx):
    zbuf[...] = jnp.zeros(zbuf.shape, F32)
    sizes = [1 << b for b in reversed(range((tmx - 1).bit_length()))]

    def visit(e, wait):
        cnt = cnt_ref[e]
        start = ps_ref[e] + cnt
        pad = lax.rem(tmx - lax.rem(cnt, tmx), tmx)
        for size in sizes:
            part = pad & size

            @pl.when(part != 0)
            def _():
                cp = pltpu.make_async_copy(zbuf.at[pl.ds(0, size)], rows_ref.at[pl.ds(start, size)], sem)
                cp.wait() if wait else cp.start()

            start = start + part

    def starts(e, c):
        visit(e, False)
        return c

    def waits(e, c):
        visit(e, True)
        return c

    lax.fori_loop(0, N_EXPERTS, starts, 0)
    lax.fori_loop(0, N_EXPERTS, waits, 0)


def _pad_rows(pstart, counts, n_rows, tmx):
    smem = pl.BlockSpec(memory_space=pltpu.SMEM)
    return pl.pallas_call(
        functools.partial(_pad_rows_kernel, tmx=tmx),
        in_specs=[smem, smem],
        out_specs=pl.BlockSpec(memory_space=pl.ANY),
        out_shape=jax.ShapeDtypeStruct((n_rows, SUBLANES, LANES), F32),
        scratch_shapes=[pltpu.VMEM((tmx // 2, SUBLANES, LANES), F32), pltpu.SemaphoreType.DMA(())],
        name="moe_pad_rows",
    )(pstart, counts)


def _dispatch_kernel(e_ref, rk_ref, ps_ref, xn_ref, rows_in_ref, rows_ref, sem, *, tm):
    del rows_in_ref

    def row_copy(j, d):
        return pltpu.make_async_copy(xn_ref.at[j], rows_ref.at[d], sem)

    def issue(jj, c):
        for u in range(DMA_UNROLL):
            j = jj * DMA_UNROLL + u
            for k in range(TOP_K):
                n = j * TOP_K + k
                row_copy(j, ps_ref[e_ref[n]] + rk_ref[n]).start()
        return c

    lax.fori_loop(0, tm // DMA_UNROLL, issue, 0)
    for _ in range(TOP_K):
        pltpu.make_async_copy(xn_ref, rows_ref.at[pl.ds(0, tm)], sem).wait()


def _dispatch(e_flat, rk_flat, pstart, xn, rows, tm):
    n = xn.shape[0]
    assert n % tm == 0
    smem = lambda shape, imap: pl.BlockSpec(shape, imap, memory_space=pltpu.SMEM)
    kern = functools.partial(_dispatch_kernel, tm=tm)
    return pl.pallas_call(
        kern,
        grid=(n // tm,),
        in_specs=[smem((tm * TOP_K,), lambda i: (i,)), smem((tm * TOP_K,), lambda i: (i,)),
                  smem(pstart.shape, lambda i: (0,)),
                  pl.BlockSpec((tm, SUBLANES, LANES), lambda i: (i, 0, 0)), pl.BlockSpec(memory_space=pl.ANY)],
        out_specs=pl.BlockSpec(memory_space=pl.ANY),
        out_shape=jax.ShapeDtypeStruct(rows.shape, rows.dtype),
        scratch_shapes=[pltpu.SemaphoreType.DMA(())],
        input_output_aliases={4: 0},
        compiler_params=_cparams(("arbitrary",)),
        name="moe_dispatch",
    )(e_flat, rk_flat, pstart, xn, rows)


def _expert_kernel(be_ref, nu_ref, x_ref, wg_ref, bg_ref, wu_ref, bu_ref, wd_ref, bd_ref, y_ref,
                   wg_s, wu_s, wd_s):
    j = pl.program_id(0)
    prev = be_ref[jnp.maximum(j - 1, 0)]

    @pl.when((j == 0) | (be_ref[j] != prev))
    def _():
        wg_s[...] = wg_ref[0].astype(BF16)
        wu_s[...] = wu_ref[0].astype(BF16)
        wd_s[...] = wd_ref[0].astype(BF16)

    @pl.when(j < nu_ref[0])
    def _():
        tm = x_ref.shape[0] // SUBLANES
        xb = _load_row_tiles(x_ref, tm).astype(BF16)
        g = jnp.dot(xb, wg_s[...], preferred_element_type=F32) + bg_ref[0]
        up = jnp.dot(xb, wu_s[...], preferred_element_type=F32) + bu_ref[0]
        g = jnp.minimum(g, SWIGLU_LIMIT)
        up = jnp.clip(up, -SWIGLU_LIMIT, SWIGLU_LIMIT)
        h = (up + 1.0) * (g * _sigmoid(SWIGLU_ALPHA * g))
        _store_row_tiles(y_ref, jnp.dot(h.astype(BF16), wd_s[...], preferred_element_type=F32) + bd_ref[0])


def _experts(blk_e, n_used, rows, p, tm):
    n_rows = rows.shape[0] // SUBLANES
    d = SUBLANES * LANES
    dff = p["w_gate"].shape[2]
    n_blk = n_rows // tm
    wsp = lambda shape: pl.BlockSpec((1,) + shape, lambda j, be, nu: (be[j], 0, 0))
    grid_spec = pltpu.PrefetchScalarGridSpec(
        num_scalar_prefetch=2,
        grid=(n_blk,),
        in_specs=[pl.BlockSpec((tm * SUBLANES, LANES), lambda j, be, nu: (j, 0)),
                  wsp((d, dff)), wsp((1, dff)), wsp((d, dff)), wsp((1, dff)),
                  wsp((dff, d)), wsp((1, d))],
        out_specs=pl.BlockSpec((tm * SUBLANES, LANES), lambda j, be, nu: (j, 0)),
        scratch_shapes=[pltpu.VMEM((d, dff), BF16), pltpu.VMEM((d, dff), BF16),
                        pltpu.VMEM((dff, d), BF16)],
    )
    return pl.pallas_call(
        _expert_kernel,
        grid_spec=grid_spec,
        out_shape=jax.ShapeDtypeStruct((n_rows * SUBLANES, LANES), F32),
        compiler_params=_cparams(("arbitrary",)),
        name="moe_experts",
    )(blk_e, n_used, rows, p["w_gate"], p["b_gate"], p["w_up"], p["b_up"], p["w_down"], p["b_down"])


def _combine_kernel(e_ref, rk_ref, en_ref, rkn_ref, ps_ref, y_ref, gt_ref, h1_ref, gf_ref, out_ref,
                    buf, sems, *, tm):
    i = pl.program_id(0)
    slot = i % 2

    def gather(er, rr, sl):
        def issue(jj, c):
            for u in range(DMA_UNROLL):
                j = jj * DMA_UNROLL + u
                for k in range(TOP_K):
                    n = j * TOP_K + k
                    dst = buf.at[sl, k, pl.ds(pl.multiple_of(j * SUBLANES, SUBLANES), SUBLANES), :]
                    pltpu.make_async_copy(y_ref.at[ps_ref[er[n]] + rr[n]], dst, sems.at[sl]).start()
            return c

        lax.fori_loop(0, tm // DMA_UNROLL, issue, 0)

    @pl.when(i == 0)
    def _():
        gather(e_ref, rk_ref, 0)

    @pl.when(i + 1 < pl.num_programs(0))
    def _():
        gather(en_ref, rkn_ref, 1 - slot)

    def finish(sl):
        for k in range(TOP_K):
            pltpu.make_async_copy(buf.at[sl, k], buf.at[sl, k], sems.at[sl]).wait()
        gt = gt_ref[...]
        y = _load_row_tiles(buf.at[sl, 0], tm) * gt[:, 0:1]
        for k in range(1, TOP_K):
            y = y + _load_row_tiles(buf.at[sl, k], tm) * gt[:, k:k + 1]
        h2 = h1_ref[...] + y
        ms = jnp.mean(h2 * h2, axis=-1, keepdims=True)
        out_ref[...] = (h2 * lax.rsqrt(ms + EPS)) * gf_ref[...]

    for sl in range(2):
        pl.when(slot == sl)(functools.partial(finish, sl))


def _combine(e_flat, rk_flat, pstart, y_rows, gt, h1, gf, tm):
    n, d = h1.shape
    assert n % tm == 0
    nt = n // tm
    smem = lambda shape, imap: pl.BlockSpec(shape, imap, memory_space=pltpu.SMEM)
    row = lambda i: (i, 0)
    cur = lambda i: (i,)
    nxt = lambda i: (jnp.minimum(i + 1, nt - 1),)
    kern = functools.partial(_combine_kernel, tm=tm)
    return pl.pallas_call(
        kern,
        grid=(nt,),
        in_specs=[smem((tm * TOP_K,), cur), smem((tm * TOP_K,), cur),
                  smem((tm * TOP_K,), nxt), smem((tm * TOP_K,), nxt),
                  smem(pstart.shape, lambda i: (0,)),
                  pl.BlockSpec(memory_space=pl.ANY),
                  pl.BlockSpec((tm, LANES), row), pl.BlockSpec((tm, d), row),
                  pl.BlockSpec((1, d), lambda i: (0, 0))],
        out_specs=pl.BlockSpec((tm, d), row),
        out_shape=jax.ShapeDtypeStruct((n, d), F32),
        scratch_shapes=[pltpu.VMEM((2, TOP_K, tm * SUBLANES, LANES), F32), pltpu.SemaphoreType.DMA((2,))],
        compiler_params=_cparams(("arbitrary",)),
        name="moe_combine",
    )(e_flat, rk_flat, e_flat, rk_flat, pstart, y_rows, gt, h1, gf)


def kernel(x_prompt, x_sample, cache_k, cache_v, state_conv, state_h, page_table, meta_tokens, norm1_g, w_in, lambda_q1, lambda_k1, lambda_q2, lambda_k2, subln_g, conv_w, conv_b, rg_wa, rg_ba, rg_wx, rg_bx, rg_lambda, w_attn_proj, w_lru_proj, w_out, norm2_g, router_w, router_b, w_gate, b_gate, w_up, b_up, w_down, b_down, final_norm_g):
    depth = w_in.shape[0]
    assert depth == 1, "single-layer stack"
    b, t, d = x_prompt.shape
    bd, s_new, _ = x_sample.shape
    n_p, n_s = b * t, bd * s_new
    row1 = lambda a: a.reshape(1, -1)

    w_in_bf = w_in[0].astype(BF16)
    g1 = row1(norm1_g[0])
    lv = jnp.stack([lambda_q1[0], lambda_k1[0], lambda_q2[0], lambda_k2[0]])
    sg = row1(subln_g[0])
    lru_p = dict(conv_w=conv_w[0], conv_b=row1(conv_b[0]), wa=rg_wa[0].astype(BF16), ba=row1(rg_ba[0]),
                 wx=rg_wx[0].astype(BF16), bx=row1(rg_bx[0]), lam=row1(rg_lambda[0]))
    rw = jnp.zeros((d, LANES), BF16).at[:, :N_EXPERTS].set(router_w[0].astype(BF16))
    rb = jnp.zeros((1, LANES), F32).at[0, :N_EXPERTS].set(router_b[0])
    post_p = dict(w_attn_proj=w_attn_proj[0].astype(BF16), w_lru_proj=w_lru_proj[0].astype(BF16),
                  w_out=w_out[0].astype(BF16), norm2_g=row1(norm2_g[0]), router_w=rw, router_b=rb)
    exp_p = dict(w_gate=w_gate[0], b_gate=b_gate[0][:, None, :], w_up=w_up[0], b_up=b_up[0][:, None, :],
                 w_down=w_down[0], b_down=b_down[0][:, None, :])

    xp = x_prompt.reshape(n_p, d)
    xs = x_sample.reshape(n_s, d)
    qbP, kbP, vbP, k3P, v3P, xrP, yrP, gaP, grP = _inproj(xp, g1, w_in_bf, _tile(t, ROW_TILE), (t, N_META))
    qbS, kbS, vbS, k3S, v3S, xrS, yrS, gaS, grS = _inproj(xs, g1, w_in_bf, _tile(n_s, ROW_TILE))
    _, kbM, vbM, k3M, v3M, xrM, yrM, _, _ = _inproj(meta_tokens, g1, w_in_bf, N_META)

    oP = _attn_prompt(lv, qbP.reshape(b, t, d), kbP.reshape(b, t, d), vbP.reshape(b, t, d), kbM, vbM, sg)
    n_pool, page = cache_k.shape[1], cache_k.shape[2]
    oS = _attn_sample(page_table, lv, qbS.astype(F32).reshape(bd, s_new, d),
                      k3S.reshape(bd, s_new * N_HEADS, HEAD_W), v3S.reshape(bd, s_new * N_HEADS, HEAD_W), sg,
                      cache_k[0].reshape(n_pool, page * N_HEADS, HEAD_W),
                      cache_v[0].reshape(n_pool, page * N_HEADS, HEAD_W))

    gb = LRU_GB
    zc = jnp.zeros((gb, CONV_W - 1, d), F32)
    zh = jnp.zeros((gb, d), F32)
    bc = lambda a: jnp.broadcast_to(a[None], (gb,) + a.shape)
    _, hM = _lru(bc(xrM), bc(yrM), zc, zh, lru_p, N_META, N_META)
    cbP = jnp.broadcast_to(xrM[None, N_META - (CONV_W - 1):], (b, CONV_W - 1, d))
    h0P = jnp.broadcast_to(hM[0:1], (b, d))
    xrP3 = xrP.reshape(b, t, d)
    hgP, hlP = _lru(xrP3, yrP.reshape(b, t, d), cbP, h0P, lru_p, _tile(t, LRU_TT), _tile(t, LRU_TT))
    t_pad = SUBLANES
    padt = lambda a: jnp.pad(a.reshape(bd, s_new, d), ((0, 0), (0, t_pad - s_new), (0, 0)))
    xrS3 = xrS.reshape(bd, s_new, d)
    hgS, hlS = _lru(padt(xrS), padt(yrS), state_conv[0], state_h[0], lru_p, t_pad, s_new)
    hgS = hgS[:, :s_new].reshape(n_s, d)

    cnt0 = jnp.zeros((1, LANES), F32)
    h1P, xnP, eiP, gtP, rkP, cntP = _post(oP.reshape(n_p, d), hgP.reshape(n_p, d), gaP, grP, xp, post_p,
                                          cnt0, _tile(n_p, ROW_TILE))
    h1S, xnS, eiS, gtS, rkS, cnt = _post(oS.reshape(n_s, d).astype(BF16), hgS, gaS, grS, xs, post_p,
                                         cntP, _tile(n_s, ROW_TILE))

    n_tok = n_p + n_s
    tmx = MOE_TILE
    counts = cnt[0, :N_EXPERTS].astype(I32)
    padded = (counts + tmx - 1) // tmx * tmx
    pend = jnp.cumsum(padded)
    pstart = (pend - padded).astype(I32)
    n_rows = -(-(n_tok * TOP_K + N_EXPERTS * (tmx - 1)) // tmx) * tmx
    n_blk = n_rows // tmx
    blk_start = jnp.arange(n_blk, dtype=I32) * tmx
    blk_e = jnp.minimum(jnp.sum((pend[None, :] <= blk_start[:, None]).astype(I32), axis=1), N_EXPERTS - 1)
    n_used = (pend[-1:] // tmx).astype(I32)
    flat = lambda a: a[:, :TOP_K].reshape(-1)

    tiles = lambda a: a.reshape(-1, SUBLANES, LANES)
    rows = _pad_rows(pstart, counts, n_rows, tmx)
    rows = _dispatch(flat(eiP), flat(rkP), pstart, tiles(xnP), rows, _tile(n_p, DISPATCH_TILE))
    rows = _dispatch(flat(eiS), flat(rkS), pstart, tiles(xnS), rows, _tile(n_s, DISPATCH_TILE))
    y_rows = tiles(_experts(blk_e, n_used, rows.reshape(-1, LANES), exp_p, tmx))
    gf = row1(final_norm_g)
    yP = _combine(flat(eiP), flat(rkP), pstart, y_rows, gtP, h1P, gf, _tile(n_p, COMBINE_TILE))
    yS = _combine(flat(eiS), flat(rkS), pstart, y_rows, gtS, h1S, gf, _tile(n_s, COMBINE_TILE))

    def with_meta(x3, m3):
        return _fill_prefix(m3, x3, b).reshape(1, b, t + N_META, N_HEADS, HEAD_W)

    nc = CONV_W - 1
    return (yP.reshape(b, t, d), yS.reshape(bd, s_new, d),
            with_meta(k3P, k3M), with_meta(v3P, v3M),
            xrP3[:, t - nc:][None], hlP[None],
            k3S.reshape(1, bd, s_new, N_HEADS, HEAD_W), v3S.reshape(1, bd, s_new, N_HEADS, HEAD_W),
            xrS3[:, s_new - nc:][None], hlS[None])
```

```python
import functools
import math

import jax
import jax.numpy as jnp
from jax import lax
from jax.experimental import pallas as pl
from jax.experimental.pallas import tpu as pltpu

F32 = jnp.float32
BF16 = jnp.bfloat16
I32 = jnp.int32

LANES = 128
SUBLANES = 8
N_HEADS = 8
HEAD_W = 128
MAP_W = 64
N_META = 16
CONV_W = 4
LRU_BLOCKS = 4
LRU_C = 8.0
N_EXPERTS = 32
TOP_K = 4
SWIGLU_LIMIT = 7.0
SWIGLU_ALPHA = 1.702
EPS = 1e-6
SUBLN_EPS = 1e-5
LAM_INIT = 0.8 - 0.6 * math.exp(-0.3 * 0)
NEG_BIG = -1e30
VMEM_LIMIT = 56 * 1024 * 1024

ROW_TILE = 256
MOE_TILE = 512
ATT_TILE = 512
ATT_HEADS = 4
ATT_STRIP = 32
LRU_TT = 64
LRU_GB = 8
PAGES_PER_STEP = 16
SOFTMAX_SETS = 4
SOFTMAX_STRIP = 16
DISPATCH_TILE = 512
COMBINE_TILE = 256
DMA_UNROLL = 4


def _cparams(sem):
    return pltpu.CompilerParams(dimension_semantics=sem, vmem_limit_bytes=VMEM_LIMIT)


def _tile(n, pref):
    t = min(pref, n)
    while n % t:
        t //= 2
    return t


def _store_row_tiles(ref, val):
    n = val.shape[0]
    for s in range(SUBLANES):
        ref[pl.ds(s, n, stride=SUBLANES), :] = val[:, s * LANES:(s + 1) * LANES]


def _load_row_tiles(ref, n):
    return jnp.concatenate([ref[pl.ds(s, n, stride=SUBLANES), :] for s in range(SUBLANES)], axis=-1)


def _sigmoid(x):
    return 1.0 / (1.0 + jnp.exp(-x))


def _expm1(x):
    u = jnp.exp(x)
    um1 = u - 1.0
    return jnp.where(u == 1.0, x, jnp.where(um1 == -1.0, -1.0, um1 * x / jnp.log(u)))


def _lam_from(lv):
    a = jnp.sum(lv[0:1, :] * lv[1:2, :], axis=-1, keepdims=True)
    b = jnp.sum(lv[2:3, :] * lv[3:4, :], axis=-1, keepdims=True)
    return jnp.exp(a) - jnp.exp(b) + LAM_INIT


def _subln(o, g):
    ms = jnp.mean(o * o, axis=-1, keepdims=True)
    return ((o * lax.rsqrt(ms + SUBLN_EPS)) * g) * (1.0 - LAM_INIT)


def _inproj_kernel(x_ref, g_ref, w_ref, qb_ref, kb_ref, vb_ref, k3_ref, v3_ref,
                   xr_ref, yr_ref, ga_ref, gr_ref):
    tm, d = x_ref.shape
    x = x_ref[...]
    ms = jnp.mean(x * x, axis=-1, keepdims=True)
    ub = ((x * lax.rsqrt(ms + EPS)) * g_ref[...]).astype(BF16)

    def proj(j):
        return jnp.dot(ub, w_ref[:, j * d:(j + 1) * d], preferred_element_type=F32)

    q = proj(0)
    qb_ref[...] = (q * (MAP_W ** -0.5)).astype(BF16)
    k = proj(1)
    kb_ref[...] = k.astype(BF16)
    _store_row_tiles(k3_ref, k)
    v = proj(2)
    vb_ref[...] = v.astype(BF16)
    _store_row_tiles(v3_ref, v)
    xr_ref[...] = proj(3)
    yr_ref[...] = proj(4)
    ga_ref[...] = proj(5)
    gr_ref[...] = proj(6)


def _inproj(x, g, w_bf, tm, seq_prefix=None):
    r, d = x.shape
    assert r % tm == 0
    row = lambda i: (i, 0)
    f32o = jax.ShapeDtypeStruct((r, d), F32)
    bfo = jax.ShapeDtypeStruct((r, d), BF16)
    blk = pl.BlockSpec((tm, d), row)
    if seq_prefix is None:
        h3o = jax.ShapeDtypeStruct((r * N_HEADS, HEAD_W), F32)
        blk3 = pl.BlockSpec((tm * N_HEADS, HEAD_W), row)
    else:
        t, n_pre = seq_prefix
        assert t % tm == 0
        tiles = t // tm
        h3o = jax.ShapeDtypeStruct((r // t * (t + n_pre) * N_HEADS, HEAD_W), F32)
        blk3 = pl.BlockSpec(
            (pl.Element(tm * N_HEADS), pl.Element(HEAD_W)),
            lambda i: (((i // tiles) * (t + n_pre) + n_pre + (i % tiles) * tm) * N_HEADS, 0))
    return pl.pallas_call(
        _inproj_kernel,
        grid=(r // tm,),
        in_specs=[blk, pl.BlockSpec((1, d), lambda i: (0, 0)),
                  pl.BlockSpec(w_bf.shape, lambda i: (0, 0), pipeline_mode=pl.Buffered(1))],
        out_specs=[blk, blk, blk, blk3, blk3, blk, blk, blk, blk],
        out_shape=[bfo, bfo, bfo, h3o, h3o, f32o, f32o, f32o, f32o],
        compiler_params=_cparams(("parallel",)),
        name="inproj",
    )(x, g, w_bf)


def _fill_prefix_kernel(m_ref, buf_ref, o_ref):
    del buf_ref
    o_ref[...] = m_ref[...]


def _fill_prefix(m3, buf, n_seq):
    pre = m3.shape[0]
    stride = buf.shape[0] // n_seq
    assert stride % pre == 0
    return pl.pallas_call(
        _fill_prefix_kernel,
        grid=(n_seq,),
        in_specs=[pl.BlockSpec(m3.shape, lambda s: (0, 0)), pl.BlockSpec(memory_space=pl.ANY)],
        out_specs=pl.BlockSpec(m3.shape, lambda s: (s * (stride // pre), 0)),
        out_shape=jax.ShapeDtypeStruct(buf.shape, buf.dtype),
        input_output_aliases={1: 0},
        compiler_params=_cparams(("arbitrary",)),
        name="fill_prefix",
    )(m3, buf)


def _attn_prompt_kernel(lv_ref, q_ref, k_ref, v_ref, km_ref, vm_ref, g_ref, o_ref,
                        qq_scr, s_scr, p_scr, m_scr, l_scr, a_scr, acc_scr, *, tq, hp, rb):
    qi = pl.program_id(2)
    rows = 2 * tq
    nc = tq // LANES
    lam = _lam_from(lv_ref[...])
    lane = lax.broadcasted_iota(I32, (tq, HEAD_W), 1)
    heads = [slice(h * HEAD_W, (h + 1) * HEAD_W) for h in range(hp)]
    wide = lambda x: jnp.broadcast_to(x, (x.shape[0], LANES))

    def scores(qq, kc):
        return lax.dot_general(qq, kc, (((1,), (1,)), ((), ())), preferred_element_type=F32)

    for h, sl in enumerate(heads):
        q = q_ref[0, :, sl]
        zero = jnp.zeros_like(q)
        qq_scr[h, :tq, :] = jnp.where(lane < MAP_W, q, zero)
        qq_scr[h, tq:, :] = jnp.where(lane >= MAP_W, q, zero)
        s0 = scores(qq_scr[h], km_ref[:, sl])
        m = jnp.max(s0, axis=-1, keepdims=True)
        p0 = jnp.exp(s0 - m)
        m_scr[h] = wide(m)
        l_scr[h] = wide(jnp.sum(p0, axis=-1, keepdims=True))
        acc_scr[h] = jnp.dot(p0.astype(BF16), vm_ref[:, sl], preferred_element_type=F32)

    def key_tile(h, sl, off, causal):
        s_scr[h] = scores(qq_scr[h], k_ref[0, pl.ds(off, tq), sl])
        for r in range(rows // rb):
            rs = slice(r * rb, (r + 1) * rb)
            q0 = (r * rb) % tq
            live = min(nc, (q0 + rb - 1) // LANES + 1) if causal else nc
            cols = [s_scr[h, rs, c * LANES:(c + 1) * LANES] for c in range(live)]
            for c in range(live, nc):
                p_scr[h, rs, c * LANES:(c + 1) * LANES] = jnp.zeros((rb, LANES), BF16)
            if causal:
                qpos = lax.broadcasted_iota(I32, (rb, LANES), 0) + q0
                kpos = lax.broadcasted_iota(I32, (rb, LANES), 1)
                cols = [jnp.where(kpos + c * LANES <= qpos, s, -jnp.inf) for c, s in enumerate(cols)]
            mx = cols[0]
            for s in cols[1:]:
                mx = jnp.maximum(mx, s)
            m_prev = m_scr[h, rs, :]
            m_new = jnp.maximum(m_prev, jnp.max(mx, axis=-1, keepdims=True))
            alpha = jnp.exp(m_prev - m_new)
            psum = None
            for c, s in enumerate(cols):
                p = jnp.exp(s - m_new)
                psum = p if psum is None else psum + p
                p_scr[h, rs, c * LANES:(c + 1) * LANES] = p.astype(BF16)
            l_scr[h, rs, :] = alpha * l_scr[h, rs, :] + jnp.sum(psum, axis=-1, keepdims=True)
            m_scr[h, rs, :] = m_new
            a_scr[h, rs, :] = alpha
        acc_scr[h] = a_scr[h] * acc_scr[h] + jnp.dot(p_scr[h], v_ref[0, pl.ds(off, tq), sl],
                                                       preferred_element_type=F32)

    def body(j, c):
        off = pl.multiple_of(j * tq, tq)
        for h, sl in enumerate(heads):
            key_tile(h, sl, off, False)
        return c

    lax.fori_loop(0, qi, body, 0)

    off = pl.multiple_of(qi * tq, tq)
    for h, sl in enumerate(heads):
        key_tile(h, sl, off, True)
        o = acc_scr[h] / l_scr[h]
        o = o[:tq] - lam * o[tq:]
        o_ref[0, :, sl] = _subln(o, g_ref[...]).astype(o_ref.dtype)


def _attn_prompt(lv, qb, kb, vb, kmb, vmb, subln_g):
    b, t, d = qb.shape
    tq = _tile(t, ATT_TILE)
    hp = ATT_HEADS
    w = hp * HEAD_W
    assert tq % LANES == 0 and tq % ATT_STRIP == 0
    rows = 2 * tq
    kern = functools.partial(_attn_prompt_kernel, tq=tq, hp=hp, rb=ATT_STRIP)
    stat = pltpu.VMEM((hp, rows, LANES), F32)
    return pl.pallas_call(
        kern,
        grid=(b, N_HEADS // hp, t // tq),
        scratch_shapes=[pltpu.VMEM((hp, rows, HEAD_W), BF16), pltpu.VMEM((hp, rows, tq), F32),
                        pltpu.VMEM((hp, rows, tq), BF16), stat, stat, stat, stat],
        in_specs=[pl.BlockSpec(lv.shape, lambda bi, h, i: (0, 0)),
                  pl.BlockSpec((1, tq, w), lambda bi, h, i: (bi, i, h)),
                  pl.BlockSpec((1, t, w), lambda bi, h, i: (bi, 0, h)),
                  pl.BlockSpec((1, t, w), lambda bi, h, i: (bi, 0, h)),
                  pl.BlockSpec((N_META, w), lambda bi, h, i: (0, h)),
                  pl.BlockSpec((N_META, w), lambda bi, h, i: (0, h)),
                  pl.BlockSpec((1, HEAD_W), lambda bi, h, i: (0, 0))],
        out_specs=pl.BlockSpec((1, tq, w), lambda bi, h, i: (bi, i, h)),
        out_shape=jax.ShapeDtypeStruct((b, t, d), BF16),
        compiler_params=_cparams(("parallel", "parallel", "arbitrary")),
        name="attn_prompt",
    )(lv, qb, kb, vb, kmb, vmb, subln_g)


def _attn_sample_kernel(pt_ref, lv_ref, q_ref, kn_ref, vn_ref, g_ref, *rest, pp, s_new):
    kpages = rest[:pp]
    vpages = rest[pp:2 * pp]
    o_ref = rest[2 * pp]
    qq_scr, s_scr, p_scr, m_scr, l_scr, a_scr, acc_scr = rest[2 * pp + 1:]
    g = pl.program_id(1)
    ng = pl.num_programs(1)
    hr = 2 * s_new
    rows = N_HEADS * hr

    def head_bias(shape):
        row = lax.broadcasted_iota(I32, shape, 0)
        col = lax.broadcasted_iota(I32, shape, 1)
        return row, col, (col % N_HEADS) == (row // hr)

    @pl.when(g == 0)
    def _():
        lane = lax.broadcasted_iota(I32, (s_new, HEAD_W), 1)
        parts = []
        for h in range(N_HEADS):
            qh = q_ref[0, :, h * HEAD_W:(h + 1) * HEAD_W]
            parts += [jnp.where(lane < MAP_W, qh, 0.0), jnp.where(lane >= MAP_W, qh, 0.0)]
        qq_scr[...] = jnp.concatenate(parts, axis=0).astype(BF16)
        m_scr[...] = jnp.full(m_scr.shape, -jnp.inf, F32)
        l_scr[...] = jnp.zeros(l_scr.shape, F32)
        acc_scr[...] = jnp.zeros(acc_scr.shape, F32)

    def scores(kc):
        return lax.dot_general(qq_scr[...], kc, (((1,), (1,)), ((), ())), preferred_element_type=F32)

    def update(a, s_list, v_list):
        m_prev = m_scr[a]
        s_max = s_list[0]
        for s in s_list[1:]:
            s_max = jnp.maximum(s_max, s)
        m_new = jnp.maximum(m_prev, jnp.max(s_max, axis=-1, keepdims=True))
        alpha = jnp.exp(m_prev - m_new)
        l_add = None
        pv = None
        for s, vc in zip(s_list, v_list):
            p = jnp.exp(s - m_new[:, :1])
            ps = jnp.sum(p, axis=-1, keepdims=True)
            l_add = ps if l_add is None else l_add + ps
            d = jnp.dot(p.astype(BF16), vc, preferred_element_type=F32)
            pv = d if pv is None else pv + d
        m_scr[a] = m_new
        l_scr[a] = alpha * l_scr[a] + l_add
        acc_scr[a] = alpha * acc_scr[a] + pv

    def update_pages(a, pages):
        for i in pages:
            s_scr[i] = scores(kpages[i][...].astype(BF16))
        blocks = [(i, c) for i in pages for c in range(s_scr.shape[2] // LANES)]
        for r0 in range(0, rows, SOFTMAX_STRIP):
            rs = slice(r0, r0 + SOFTMAX_STRIP)
            head = (lax.broadcasted_iota(I32, (SOFTMAX_STRIP, LANES), 0) + r0) // hr
            lane = lax.broadcasted_iota(I32, (SOFTMAX_STRIP, LANES), 1)
            bias = jnp.where(lane % N_HEADS == head, 0.0, -jnp.inf)
            mx = None
            for i, c in blocks:
                s = s_scr[i, rs, c * LANES:(c + 1) * LANES] + bias
                mx = s if mx is None else jnp.maximum(mx, s)
            m_prev = m_scr[a, rs, :]
            m_new = jnp.maximum(m_prev, jnp.max(mx, axis=-1, keepdims=True))
            alpha = jnp.exp(m_prev - m_new)
            psum = None
            for i, c in blocks:
                p = jnp.exp(s_scr[i, rs, c * LANES:(c + 1) * LANES] + bias - m_new)
                psum = p if psum is None else psum + p
                p_scr[i, rs, c * LANES:(c + 1) * LANES] = p.astype(BF16)
            m_scr[a, rs, :] = m_new
            l_scr[a, rs, :] = alpha * l_scr[a, rs, :] + jnp.sum(psum, axis=-1, keepdims=True)
            a_scr[a, rs, :] = alpha
        pv = None
        for i in pages:
            d = jnp.dot(p_scr[i], vpages[i][...].astype(BF16), preferred_element_type=F32)
            pv = d if pv is None else pv + d
        acc_scr[a] = a_scr[a] * acc_scr[a] + pv

    n_acc = m_scr.shape[0]
    per = pp // n_acc
    for a in range(n_acc):
        update_pages(a, range(a * per, (a + 1) * per))

    @pl.when(g == ng - 1)
    def _():
        lam = _lam_from(lv_ref[...])
        row, col, same = head_bias((rows, s_new * N_HEADS))
        causal = (col // N_HEADS) <= ((row % hr) % s_new)
        s = jnp.where(same & causal, scores(kn_ref[0].astype(BF16)), -jnp.inf)
        update(0, [s], [vn_ref[0].astype(BF16)])
        m = m_scr[0]
        for a in range(1, n_acc):
            m = jnp.maximum(m, m_scr[a])
        l = jnp.zeros_like(m)
        acc = jnp.zeros(acc_scr.shape[1:], F32)
        for a in range(n_acc):
            w = jnp.exp(m_scr[a] - m)
            l = l + w * l_scr[a]
            acc = acc + w * acc_scr[a]
        o = acc / l
        for h in range(N_HEADS):
            oh = o[h * hr:h * hr + s_new] - lam * o[h * hr + s_new:(h + 1) * hr]
            o_ref[0, :, h * HEAD_W:(h + 1) * HEAD_W] = _subln(oh, g_ref[...])


def _attn_sample(page_table, lv, q, kn, vn, subln_g, cache_k, cache_v):
    bd, s_new, d = q.shape
    n_pages = page_table.shape[1]
    page_rows = cache_k.shape[1]
    pp = _tile(n_pages, PAGES_PER_STEP)
    n_acc = _tile(pp, SOFTMAX_SETS)
    rows = N_HEADS * 2 * s_new
    pt = page_table.reshape(-1)
    seq = lambda b, g, pt: (b, 0, 0)

    def page_spec(p):
        return pl.BlockSpec((None, page_rows, HEAD_W),
                            lambda b, g, pt, p=p: (pt[b * n_pages + g * pp + p], 0, 0))

    grid_spec = pltpu.PrefetchScalarGridSpec(
        num_scalar_prefetch=1,
        grid=(bd, n_pages // pp),
        in_specs=[pl.BlockSpec(lv.shape, lambda b, g, pt: (0, 0)),
                  pl.BlockSpec((1, s_new, d), seq),
                  pl.BlockSpec((1, s_new * N_HEADS, HEAD_W), seq),
                  pl.BlockSpec((1, s_new * N_HEADS, HEAD_W), seq),
                  pl.BlockSpec((1, HEAD_W), lambda b, g, pt: (0, 0))]
                 + [page_spec(p) for p in range(pp)] + [page_spec(p) for p in range(pp)],
        out_specs=pl.BlockSpec((1, s_new, d), seq),
        scratch_shapes=[pltpu.VMEM((rows, HEAD_W), BF16),
                        pltpu.VMEM((pp, rows, page_rows), F32),
                        pltpu.VMEM((pp, rows, page_rows), BF16),
                        pltpu.VMEM((n_acc, rows, LANES), F32),
                        pltpu.VMEM((n_acc, rows, LANES), F32),
                        pltpu.VMEM((n_acc, rows, LANES), F32),
                        pltpu.VMEM((n_acc, rows, HEAD_W), F32)],
    )
    kern = functools.partial(_attn_sample_kernel, pp=pp, s_new=s_new)
    return pl.pallas_call(
        kern,
        grid_spec=grid_spec,
        out_shape=jax.ShapeDtypeStruct((bd, s_new, d), F32),
        compiler_params=_cparams(("parallel", "arbitrary")),
        name="attn_sample",
    )(pt, lv, q, kn, vn, subln_g, *([cache_k] * pp), *([cache_v] * pp))


def _lru_kernel(xr_ref, yr_ref, cb0_ref, h0_ref, cw_ref, cb_ref, wa_ref, ba_ref, wx_ref, bx_ref,
                lam_ref, hg_ref, hl_ref, xbuf, a_scr, u_scr, h_scr, *, n_steps):
    gb, tt, w = xr_ref.shape
    bw = w // LRU_BLOCKS
    i = pl.program_id(1)
    halo = SUBLANES

    @pl.when(i == 0)
    def _():
        xbuf[:, halo - (CONV_W - 1):halo, :] = cb0_ref[...]
        h_scr[...] = h0_ref[...]

    xbuf[:, halo:halo + tt, :] = xr_ref[...]
    xc = cb_ref[...] + xbuf[:, halo - 3:halo - 3 + tt, :] * cw_ref[0:1, :]
    for j in range(1, CONV_W):
        xc = xc + xbuf[:, halo - 3 + j:halo - 3 + j + tt, :] * cw_ref[j:j + 1, :]
    xbuf[:, halo - (CONV_W - 1):halo, :] = xbuf[:, halo + tt - (CONV_W - 1):halo + tt, :]

    xc2 = xc.reshape(gb * tt, w)
    xcb = xc2.astype(BF16)
    r_parts, i_parts = [], []
    for c in range(LRU_BLOCKS):
        xs = xcb[:, c * bw:(c + 1) * bw]
        r_parts.append(jnp.dot(xs, wa_ref[c], preferred_element_type=F32))
        i_parts.append(jnp.dot(xs, wx_ref[c], preferred_element_type=F32))
    r = _sigmoid(jnp.concatenate(r_parts, axis=-1) + ba_ref[...])
    ig = _sigmoid(jnp.concatenate(i_parts, axis=-1) + bx_ref[...])
    nl = -lam_ref[...]
    softplus = jnp.maximum(nl, 0.0) + jnp.log1p(jnp.exp(-jnp.abs(nl)))
    log_a = (-LRU_C * r) * softplus
    a_scr[...] = jnp.exp(log_a).reshape(gb, tt, w)
    mult = jnp.sqrt(-_expm1(2.0 * log_a))
    u_scr[...] = (xc2 * ig * mult).reshape(gb, tt, w)

    def step(t, h):
        h = a_scr[:, t, :] * h + u_scr[:, t, :]
        u_scr[:, t, :] = h
        return h

    h = lax.fori_loop(0, n_steps, step, h_scr[...])
    h_scr[...] = h
    hl_ref[...] = h
    y = yr_ref[...]
    gelu = y * (0.5 * (1.0 + jnp.tanh(math.sqrt(2.0 / math.pi) * (y + 0.044715 * (y * y * y)))))
    hg_ref[...] = (u_scr[...] * gelu).astype(hg_ref.dtype)


def _lru(xr, yr, cb0, h0, p, tt, n_steps):
    b, t, w = xr.shape
    gb = LRU_GB
    assert b % gb == 0 and t % tt == 0 and (n_steps == tt or t == tt)
    bw = w // LRU_BLOCKS
    blk = pl.BlockSpec((gb, tt, w), lambda g, i: (g, i, 0))
    vec = pl.BlockSpec((1, w), lambda g, i: (0, 0))
    wsp = pl.BlockSpec((LRU_BLOCKS, bw, bw), lambda g, i: (0, 0, 0))
    kern = functools.partial(_lru_kernel, n_steps=n_steps)
    return pl.pallas_call(
        kern,
        grid=(b // gb, t // tt),
        in_specs=[blk, blk,
                  pl.BlockSpec((gb, CONV_W - 1, w), lambda g, i: (g, 0, 0)),
                  pl.BlockSpec((gb, w), lambda g, i: (g, 0)),
                  pl.BlockSpec((CONV_W, w), lambda g, i: (0, 0)), vec, wsp, vec, wsp, vec, vec],
        out_specs=[blk, pl.BlockSpec((gb, w), lambda g, i: (g, 0))],
        out_shape=[jax.ShapeDtypeStruct((b, t, w), BF16), jax.ShapeDtypeStruct((b, w), F32)],
        scratch_shapes=[pltpu.VMEM((gb, tt + SUBLANES, w), F32), pltpu.VMEM((gb, tt, w), F32),
                        pltpu.VMEM((gb, tt, w), F32), pltpu.VMEM((gb, w), F32)],
        compiler_params=_cparams(("parallel", "arbitrary")),
        name="lru",
    )(xr, yr, cb0, h0, p["conv_w"], p["conv_b"], p["wa"], p["ba"], p["wx"], p["bx"], p["lam"])


def _post_kernel(o_ref, hg_ref, ga_ref, gr_ref, x_ref, wap_ref, wlp_ref, wo_ref, g2_ref, rw_ref,
                 rb_ref, cin_ref, h1_ref, xn_ref, ei_ref, gt_ref, rk_ref, cnt_ref, carry):
    tm = x_ref.shape[0]

    @pl.when(pl.program_id(0) == 0)
    def _():
        carry[...] = cin_ref[...]

    attn_out = jnp.dot(o_ref[...], wap_ref[...], preferred_element_type=F32)
    lru_out = jnp.dot(hg_ref[...], wlp_ref[...], preferred_element_type=F32)
    merged = _sigmoid(ga_ref[...]) * attn_out + _sigmoid(gr_ref[...]) * lru_out
    h1 = x_ref[...] + jnp.dot(merged.astype(BF16), wo_ref[...], preferred_element_type=F32)
    h1_ref[...] = h1
    ms = jnp.mean(h1 * h1, axis=-1, keepdims=True)
    xn = (h1 * lax.rsqrt(ms + EPS)) * g2_ref[...]
    _store_row_tiles(xn_ref, xn)

    lane = lax.broadcasted_iota(I32, (tm, LANES), 1)
    logits = jnp.dot(xn.astype(BF16), rw_ref[...], preferred_element_type=F32) + rb_ref[...]
    lg = jnp.where(lane < N_EXPERTS, logits, NEG_BIG)
    tops, idxs, hots = [], [], []
    for _ in range(TOP_K):
        mx = jnp.max(lg, axis=-1, keepdims=True)
        idx = jnp.min(jnp.where(lg == mx, lane, LANES), axis=-1, keepdims=True)
        hot = lane == idx
        tops.append(mx); idxs.append(idx); hots.append(hot)
        lg = jnp.where(hot, NEG_BIG, lg)
    exps = [jnp.exp(tv - tops[0]) for tv in tops]
    den = exps[0] + exps[1] + exps[2] + exps[3]

    sel = jnp.zeros((tm, LANES), F32)
    for hot in hots:
        sel = sel + jnp.where(hot, 1.0, 0.0)
    r_i = lax.broadcasted_iota(I32, (tm, tm), 0)
    c_i = lax.broadcasted_iota(I32, (tm, tm), 1)
    ltri = jnp.where(c_i < r_i, 1.0, 0.0).astype(BF16)
    base = jnp.dot(ltri, sel.astype(BF16), preferred_element_type=F32) + carry[...]
    new_carry = carry[...] + jnp.sum(sel, axis=0, keepdims=True)
    carry[...] = new_carry
    cnt_ref[...] = new_carry

    ei = jnp.zeros((tm, LANES), I32)
    gt = jnp.zeros((tm, LANES), F32)
    rk = jnp.zeros((tm, LANES), I32)
    for k in range(TOP_K):
        rank = jnp.sum(jnp.where(hots[k], base, 0.0), axis=-1, keepdims=True)
        ei = jnp.where(lane == k, idxs[k], ei)
        gt = jnp.where(lane == k, exps[k] / den, gt)
        rk = jnp.where(lane == k, rank.astype(I32), rk)
    ei_ref[...] = ei
    gt_ref[...] = gt
    rk_ref[...] = rk


def _post(o, hg, ga, gr, x, p, cnt_in, tm):
    r, d = x.shape
    assert r % tm == 0
    row = lambda i: (i, 0)
    fix = lambda i: (0, 0)
    blk = pl.BlockSpec((tm, d), row)
    wsp = pl.BlockSpec((d, d), fix, pipeline_mode=pl.Buffered(1))
    lsp = pl.BlockSpec((tm, LANES), row)
    vec = pl.BlockSpec((1, LANES), fix)
    return pl.pallas_call(
        _post_kernel,
        grid=(r // tm,),
        in_specs=[blk, blk, blk, blk, blk, wsp, wsp, wsp, pl.BlockSpec((1, d), fix),
                  pl.BlockSpec((d, LANES), fix), vec, vec],
        out_specs=[blk, pl.BlockSpec((tm * SUBLANES, LANES), row), lsp, lsp, lsp, vec],
        out_shape=[jax.ShapeDtypeStruct((r, d), F32), jax.ShapeDtypeStruct((r * SUBLANES, LANES), F32),
                   jax.ShapeDtypeStruct((r, LANES), I32), jax.ShapeDtypeStruct((r, LANES), F32),
                   jax.ShapeDtypeStruct((r, LANES), I32), jax.ShapeDtypeStruct((1, LANES), F32)],
        scratch_shapes=[pltpu.VMEM((1, LANES), F32)],
        compiler_params=_cparams(("arbitrary",)),
        name="post_router",
    )(o, hg, ga, gr, x, p["w_attn_proj"], p["w_lru_proj"], p["w_out"], p["norm2_g"],
      p["router_w"], p["router_b"], cnt_in)


def _pad_rows_kernel(ps_ref, cnt_ref, rows_ref, zbuf, sem, *, tmx):
    zbuf[...] = jnp.zeros(zbuf.shape, F32)
    sizes = [1 << b for b in reversed(range((tmx - 1).bit_length()))]

    def visit(e, wait):
        cnt = cnt_ref[e]
        start = ps_ref[e] + cnt
        pad = lax.rem(tmx - lax.rem(cnt, tmx), tmx)
        for size in sizes:
            part = pad & size

            @pl.when(part != 0)
            def _():
                cp = pltpu.make_async_copy(zbuf.at[pl.ds(0, size)], rows_ref.at[pl.ds(start, size)], sem)
                cp.wait() if wait else cp.start()

            start = start + part

    def starts(e, c):
        visit(e, False)
        return c

    def waits(e, c):
        visit(e, True)
        return c

    lax.fori_loop(0, N_EXPERTS, starts, 0)
    lax.fori_loop(0, N_EXPERTS, waits, 0)


def _pad_rows(pstart, counts, n_rows, tmx):
    smem = pl.BlockSpec(memory_space=pltpu.SMEM)
    return pl.pallas_call(
        functools.partial(_pad_rows_kernel, tmx=tmx),
        in_specs=[smem, smem],
        out_specs=pl.BlockSpec(memory_space=pl.ANY),
        out_shape=jax.ShapeDtypeStruct((n_rows, SUBLANES, LANES), F32),
        scratch_shapes=[pltpu.VMEM((tmx // 2, SUBLANES, LANES), F32), pltpu.SemaphoreType.DMA(())],
        name="moe_pad_rows",
    )(pstart, counts)


def _dispatch_kernel(e_ref, rk_ref, ps_ref, xn_ref, rows_in_ref, rows_ref, sem, *, tm):
    del rows_in_ref

    def row_copy(j, d):
        return pltpu.make_async_copy(xn_ref.at[j], rows_ref.at[d], sem)

    def issue(jj, c):
        for u in range(DMA_UNROLL):
            j = jj * DMA_UNROLL + u
            for k in range(TOP_K):
                n = j * TOP_K + k
                row_copy(j, ps_ref[e_ref[n]] + rk_ref[n]).start()
        return c

    lax.fori_loop(0, tm // DMA_UNROLL, issue, 0)
    for _ in range(TOP_K):
        pltpu.make_async_copy(xn_ref, rows_ref.at[pl.ds(0, tm)], sem).wait()


def _dispatch(e_flat, rk_flat, pstart, xn, rows, tm):
    n = xn.shape[0]
    assert n % tm == 0
    smem = lambda shape, imap: pl.BlockSpec(shape, imap, memory_space=pltpu.SMEM)
    kern = functools.partial(_dispatch_kernel, tm=tm)
    return pl.pallas_call(
        kern,
        grid=(n // tm,),
        in_specs=[smem((tm * TOP_K,), lambda i: (i,)), smem((tm * TOP_K,), lambda i: (i,)),
                  smem(pstart.shape, lambda i: (0,)),
                  pl.BlockSpec((tm, SUBLANES, LANES), lambda i: (i, 0, 0)), pl.BlockSpec(memory_space=pl.ANY)],
        out_specs=pl.BlockSpec(memory_space=pl.ANY),
        out_shape=jax.ShapeDtypeStruct(rows.shape, rows.dtype),
        scratch_shapes=[pltpu.SemaphoreType.DMA(())],
        input_output_aliases={4: 0},
        compiler_params=_cparams(("arbitrary",)),
        name="moe_dispatch",
    )(e_flat, rk_flat, pstart, xn, rows)


def _expert_kernel(be_ref, nu_ref, x_ref, wg_ref, bg_ref, wu_ref, bu_ref, wd_ref, bd_ref, y_ref,
                   wg_s, wu_s, wd_s):
    j = pl.program_id(0)
    prev = be_ref[jnp.maximum(j - 1, 0)]

    @pl.when((j == 0) | (be_ref[j] != prev))
    def _():
        wg_s[...] = wg_ref[0].astype(BF16)
        wu_s[...] = wu_ref[0].astype(BF16)
        wd_s[...] = wd_ref[0].astype(BF16)

    @pl.when(j < nu_ref[0])
    def _():
        tm = x_ref.shape[0] // SUBLANES
        xb = _load_row_tiles(x_ref, tm).astype(BF16)
        g = jnp.dot(xb, wg_s[...], preferred_element_type=F32) + bg_ref[0]
        up = jnp.dot(xb, wu_s[...], preferred_element_type=F32) + bu_ref[0]
        g = jnp.minimum(g, SWIGLU_LIMIT)
        up = jnp.clip(up, -SWIGLU_LIMIT, SWIGLU_LIMIT)
        h = (up + 1.0) * (g * _sigmoid(SWIGLU_ALPHA * g))
        _store_row_tiles(y_ref, jnp.dot(h.astype(BF16), wd_s[...], preferred_element_type=F32) + bd_ref[0])


def _experts(blk_e, n_used, rows, p, tm):
    n_rows = rows.shape[0] // SUBLANES
    d = SUBLANES * LANES
    dff = p["w_gate"].shape[2]
    n_blk = n_rows // tm
    wsp = lambda shape: pl.BlockSpec((1,) + shape, lambda j, be, nu: (be[j], 0, 0))
    grid_spec = pltpu.PrefetchScalarGridSpec(
        num_scalar_prefetch=2,
        grid=(n_blk,),
        in_specs=[pl.BlockSpec((tm * SUBLANES, LANES), lambda j, be, nu: (j, 0)),
                  wsp((d, dff)), wsp((1, dff)), wsp((d, dff)), wsp((1, dff)),
                  wsp((dff, d)), wsp((1, d))],
        out_specs=pl.BlockSpec((tm * SUBLANES, LANES), lambda j, be, nu: (j, 0)),
        scratch_shapes=[pltpu.VMEM((d, dff), BF16), pltpu.VMEM((d, dff), BF16),
                        pltpu.VMEM((dff, d), BF16)],
    )
    return pl.pallas_call(
        _expert_kernel,
        grid_spec=grid_spec,
        out_shape=jax.ShapeDtypeStruct((n_rows * SUBLANES, LANES), F32),
        compiler_params=_cparams(("arbitrary",)),
        name="moe_experts",
    )(blk_e, n_used, rows, p["w_gate"], p["b_gate"], p["w_up"], p["b_up"], p["w_down"], p["b_down"])


def _combine_kernel(e_ref, rk_ref, en_ref, rkn_ref, ps_ref, y_ref, gt_ref, h1_ref, gf_ref, out_ref,
                    buf, sems, *, tm):
    i = pl.program_id(0)
    slot = i % 2

    def gather(er, rr, sl):
        def issue(jj, c):
            for u in range(DMA_UNROLL):
                j = jj * DMA_UNROLL + u
                for k in range(TOP_K):
                    n = j * TOP_K + k
                    dst = buf.at[sl, k, pl.ds(pl.multiple_of(j * SUBLANES, SUBLANES), SUBLANES), :]
                    pltpu.make_async_copy(y_ref.at[ps_ref[er[n]] + rr[n]], dst, sems.at[sl]).start()
            return c

        lax.fori_loop(0, tm // DMA_UNROLL, issue, 0)

    @pl.when(i == 0)
    def _():
        gather(e_ref, rk_ref, 0)

    @pl.when(i + 1 < pl.num_programs(0))
    def _():
        gather(en_ref, rkn_ref, 1 - slot)

    def finish(sl):
        for k in range(TOP_K):
            pltpu.make_async_copy(buf.at[sl, k], buf.at[sl, k], sems.at[sl]).wait()
        gt = gt_ref[...]
        y = _load_row_tiles(buf.at[sl, 0], tm) * gt[:, 0:1]
        for k in range(1, TOP_K):
            y = y + _load_row_tiles(buf.at[sl, k], tm) * gt[:, k:k + 1]
        h2 = h1_ref[...] + y
        ms = jnp.mean(h2 * h2, axis=-1, keepdims=True)
        out_ref[...] = (h2 * lax.rsqrt(ms + EPS)) * gf_ref[...]

    for sl in range(2):
        pl.when(slot == sl)(functools.partial(finish, sl))


def _combine(e_flat, rk_flat, pstart, y_rows, gt, h1, gf, tm):
    n, d = h1.shape
    assert n % tm == 0
    nt = n // tm
    smem = lambda shape, imap: pl.BlockSpec(shape, imap, memory_space=pltpu.SMEM)
    row = lambda i: (i, 0)
    cur = lambda i: (i,)
    nxt = lambda i: (jnp.minimum(i + 1, nt - 1),)
    kern = functools.partial(_combine_kernel, tm=tm)
    return pl.pallas_call(
        kern,
        grid=(nt,),
        in_specs=[smem((tm * TOP_K,), cur), smem((tm * TOP_K,), cur),
                  smem((tm * TOP_K,), nxt), smem((tm * TOP_K,), nxt),
                  smem(pstart.shape, lambda i: (0,)),
                  pl.BlockSpec(memory_space=pl.ANY),
                  pl.BlockSpec((tm, LANES), row), pl.BlockSpec((tm, d), row),
                  pl.BlockSpec((1, d), lambda i: (0, 0))],
        out_specs=pl.BlockSpec((tm, d), row),
        out_shape=jax.ShapeDtypeStruct((n, d), F32),
        scratch_shapes=[pltpu.VMEM((2, TOP_K, tm * SUBLANES, LANES), F32), pltpu.SemaphoreType.DMA((2,))],
        compiler_params=_cparams(("arbitrary",)),
        name="moe_combine",
    )(e_flat, rk_flat, e_flat, rk_flat, pstart, y_rows, gt, h1, gf)


def kernel(x_prompt, x_sample, cache_k, cache_v, state_conv, state_h, page_table, meta_tokens, norm1_g, w_in, lambda_q1, lambda_k1, lambda_q2, lambda_k2, subln_g, conv_w, conv_b, rg_wa, rg_ba, rg_wx, rg_bx, rg_lambda, w_attn_proj, w_lru_proj, w_out, norm2_g, router_w, router_b, w_gate, b_gate, w_up, b_up, w_down, b_down, final_norm_g):
    depth = w_in.shape[0]
    assert depth == 1, "single-layer stack"
    b, t, d = x_prompt.shape
    bd, s_new, _ = x_sample.shape
    n_p, n_s = b * t, bd * s_new
    row1 = lambda a: a.reshape(1, -1)

    w_in_bf = w_in[0].astype(BF16)
    g1 = row1(norm1_g[0])
    lv = jnp.stack([lambda_q1[0], lambda_k1[0], lambda_q2[0], lambda_k2[0]])
    sg = row1(subln_g[0])
    lru_p = dict(conv_w=conv_w[0], conv_b=row1(conv_b[0]), wa=rg_wa[0].astype(BF16), ba=row1(rg_ba[0]),
                 wx=rg_wx[0].astype(BF16), bx=row1(rg_bx[0]), lam=row1(rg_lambda[0]))
    rw = jnp.zeros((d, LANES), BF16).at[:, :N_EXPERTS].set(router_w[0].astype(BF16))
    rb = jnp.zeros((1, LANES), F32).at[0, :N_EXPERTS].set(router_b[0])
    post_p = dict(w_attn_proj=w_attn_proj[0].astype(BF16), w_lru_proj=w_lru_proj[0].astype(BF16),
                  w_out=w_out[0].astype(BF16), norm2_g=row1(norm2_g[0]), router_w=rw, router_b=rb)
    exp_p = dict(w_gate=w_gate[0], b_gate=b_gate[0][:, None, :], w_up=w_up[0], b_up=b_up[0][:, None, :],
                 w_down=w_down[0], b_down=b_down[0][:, None, :])

    xp = x_prompt.reshape(n_p, d)
    xs = x_sample.reshape(n_s, d)
    qbP, kbP, vbP, k3P, v3P, xrP, yrP, gaP, grP = _inproj(xp, g1, w_in_bf, _tile(t, ROW_TILE), (t, N_META))
    qbS, kbS, vbS, k3S, v3S, xrS, yrS, gaS, grS = _inproj(xs, g1, w_in_bf, _tile(n_s, ROW_TILE))
    _, kbM, vbM, k3M, v3M, xrM, yrM, _, _ = _inproj(meta_tokens, g1, w_in_bf, N_META)

    oP = _attn_prompt(lv, qbP.reshape(b, t, d), kbP.reshape(b, t, d), vbP.reshape(b, t, d), kbM, vbM, sg)
    n_pool, page = cache_k.shape[1], cache_k.shape[2]
    oS = _attn_sample(page_table, lv, qbS.astype(F32).reshape(bd, s_new, d),
                      k3S.reshape(bd, s_new * N_HEADS, HEAD_W), v3S.reshape(bd, s_new * N_HEADS, HEAD_W), sg,
                      cache_k[0].reshape(n_pool, page * N_HEADS, HEAD_W),
                      cache_v[0].reshape(n_pool, page * N_HEADS, HEAD_W))

    gb = LRU_GB
    zc = jnp.zeros((gb, CONV_W - 1, d), F32)
    zh = jnp.zeros((gb, d), F32)
    bc = lambda a: jnp.broadcast_to(a[None], (gb,) + a.shape)
    _, hM = _lru(bc(xrM), bc(yrM), zc, zh, lru_p, N_META, N_META)
    cbP = jnp.broadcast_to(xrM[None, N_META - (CONV_W - 1):], (b, CONV_W - 1, d))
    h0P = jnp.broadcast_to(hM[0:1], (b, d))
    xrP3 = xrP.reshape(b, t, d)
    hgP, hlP = _lru(xrP3, yrP.reshape(b, t, d), cbP, h0P, lru_p, _tile(t, LRU_TT), _tile(t, LRU_TT))
    t_pad = SUBLANES
    padt = lambda a: jnp.pad(a.reshape(bd, s_new, d), ((0, 0), (0, t_pad - s_new), (0, 0)))
    xrS3 = xrS.reshape(bd, s_new, d)
    hgS, hlS = _lru(padt(xrS), padt(yrS), state_conv[0], state_h[0], lru_p, t_pad, s_new)
    hgS = hgS[:, :s_new].reshape(n_s, d)

    cnt0 = jnp.zeros((1, LANES), F32)
    h1P, xnP, eiP, gtP, rkP, cntP = _post(oP.reshape(n_p, d), hgP.reshape(n_p, d), gaP, grP, xp, post_p,
                                          cnt0, _tile(n_p, ROW_TILE))
    h1S, xnS, eiS, gtS, rkS, cnt = _post(oS.reshape(n_s, d).astype(BF16), hgS, gaS, grS, xs, post_p,
                                         cntP, _tile(n_s, ROW_TILE))

    n_tok = n_p + n_s
    tmx = MOE_TILE
    counts = cnt[0, :N_EXPERTS].astype(I32)
    padded = (counts + tmx - 1) // tmx * tmx
    pend = jnp.cumsum(padded)
    pstart = (pend - padded).astype(I32)
    n_rows = -(-(n_tok * TOP_K + N_EXPERTS * (tmx - 1)) // tmx) * tmx
    n_blk = n_rows // tmx
    blk_start = jnp.arange(n_blk, dtype=I32) * tmx
    blk_e = jnp.minimum(jnp.sum((pend[None, :] <= blk_start[:, None]).astype(I32), axis=1), N_EXPERTS - 1)
    n_used = (pend[-1:] // tmx).astype(I32)
    flat = lambda a: a[:, :TOP_K].reshape(-1)

    tiles = lambda a: a.reshape(-1, SUBLANES, LANES)
    rows = _pad_rows(pstart, counts, n_rows, tmx)
    rows = _dispatch(flat(eiP), flat(rkP), pstart, tiles(xnP), rows, _tile(n_p, DISPATCH_TILE))
    rows = _dispatch(flat(eiS), flat(rkS), pstart, tiles(xnS), rows, _tile(n_s, DISPATCH_TILE))
    y_rows = tiles(_experts(blk_e, n_used, rows.reshape(-1, LANES), exp_p, tmx))
    gf = row1(final_norm_g)
    yP = _combine(flat(eiP), flat(rkP), pstart, y_rows, gtP, h1P, gf, _tile(n_p, COMBINE_TILE))
    yS = _combine(flat(eiS), flat(rkS), pstart, y_rows, gtS, h1S, gf, _tile(n_s, COMBINE_TILE))

    def with_meta(x3, m3):
        return _fill_prefix(m3, x3, b).reshape(1, b, t + N_META, N_HEADS, HEAD_W)

    nc = CONV_W - 1
    return (yP.reshape(b, t, d), yS.reshape(bd, s_new, d),
            with_meta(k3P, k3M), with_meta(v3P, v3M),
            xrP3[:, t - nc:][None], hlP[None],
            k3S.reshape(1, bd, s_new, N_HEADS, HEAD_W), v3S.reshape(1, bd, s_new, N_HEADS, HEAD_W),
            xrS3[:, s_new - nc:][None], hlS[None])
```

```python
import functools
import math

import jax
import jax.numpy as jnp
from jax import lax
from jax.experimental import pallas as pl
from jax.experimental.pallas import tpu as pltpu

F32 = jnp.float32
BF16 = jnp.bfloat16
I32 = jnp.int32

LANES = 128
SUBLANES = 8
N_HEADS = 8
HEAD_W = 128
MAP_W = 64
N_META = 16
CONV_W = 4
LRU_BLOCKS = 4
LRU_C = 8.0
N_EXPERTS = 32
TOP_K = 4
SWIGLU_LIMIT = 7.0
SWIGLU_ALPHA = 1.702
EPS = 1e-6
SUBLN_EPS = 1e-5
LAM_INIT = 0.8 - 0.6 * math.exp(-0.3 * 0)
NEG_BIG = -1e30
VMEM_LIMIT = 56 * 1024 * 1024

ROW_TILE = 256
POST_TILE = 512
MOE_TILE = 512
ATT_TILE = 512
ATT_HEADS = 4
ATT_STRIP = 32
LRU_TT = 64
LRU_GB = 8
PAGES_PER_STEP = 16
SOFTMAX_SETS = 4
SOFTMAX_STRIP = 16
DISPATCH_TILE = 512
COMBINE_TILE = 256
DMA_UNROLL = 4


def _cparams(sem):
    return pltpu.CompilerParams(dimension_semantics=sem, vmem_limit_bytes=VMEM_LIMIT)


def _tile(n, pref):
    t = min(pref, n)
    while n % t:
        t //= 2
    return t


def _store_row_tiles(ref, val):
    n = val.shape[0]
    for s in range(SUBLANES):
        ref[pl.ds(s, n, stride=SUBLANES), :] = val[:, s * LANES:(s + 1) * LANES]


def _load_row_tiles(ref, n):
    return jnp.concatenate([ref[pl.ds(s, n, stride=SUBLANES), :] for s in range(SUBLANES)], axis=-1)


def _sigmoid(x):
    return 1.0 / (1.0 + jnp.exp(-x))


def _expm1(x):
    u = jnp.exp(x)
    um1 = u - 1.0
    return jnp.where(u == 1.0, x, jnp.where(um1 == -1.0, -1.0, um1 * x / jnp.log(u)))


def _lam_from(lv):
    a = jnp.sum(lv[0:1, :] * lv[1:2, :], axis=-1, keepdims=True)
    b = jnp.sum(lv[2:3, :] * lv[3:4, :], axis=-1, keepdims=True)
    return jnp.exp(a) - jnp.exp(b) + LAM_INIT


def _subln(o, g):
    ms = jnp.mean(o * o, axis=-1, keepdims=True)
    return ((o * lax.rsqrt(ms + SUBLN_EPS)) * g) * (1.0 - LAM_INIT)


def _inproj_kernel(x_ref, g_ref, w_ref, qb_ref, kb_ref, vb_ref, k3_ref, v3_ref,
                   xr_ref, yr_ref, ga_ref, gr_ref):
    tm, d = x_ref.shape
    x = x_ref[...]
    ms = jnp.mean(x * x, axis=-1, keepdims=True)
    ub = ((x * lax.rsqrt(ms + EPS)) * g_ref[...]).astype(BF16)

    def proj(j):
        return jnp.dot(ub, w_ref[:, j * d:(j + 1) * d], preferred_element_type=F32)

    q = proj(0)
    qb_ref[...] = (q * (MAP_W ** -0.5)).astype(BF16)
    k = proj(1)
    kb_ref[...] = k.astype(BF16)
    _store_row_tiles(k3_ref, k)
    v = proj(2)
    vb_ref[...] = v.astype(BF16)
    _store_row_tiles(v3_ref, v)
    xr_ref[...] = proj(3)
    yr_ref[...] = proj(4)
    ga_ref[...] = proj(5)
    gr_ref[...] = proj(6)


def _inproj(x, g, w_bf, tm, seq_prefix=None):
    r, d = x.shape
    assert r % tm == 0
    row = lambda i: (i, 0)
    f32o = jax.ShapeDtypeStruct((r, d), F32)
    bfo = jax.ShapeDtypeStruct((r, d), BF16)
    blk = pl.BlockSpec((tm, d), row)
    if seq_prefix is None:
        h3o = jax.ShapeDtypeStruct((r * N_HEADS, HEAD_W), F32)
        blk3 = pl.BlockSpec((tm * N_HEADS, HEAD_W), row)
    else:
        t, n_pre = seq_prefix
        assert t % tm == 0
        tiles = t // tm
        h3o = jax.ShapeDtypeStruct((r // t * (t + n_pre) * N_HEADS, HEAD_W), F32)
        blk3 = pl.BlockSpec(
            (pl.Element(tm * N_HEADS), pl.Element(HEAD_W)),
            lambda i: (((i // tiles) * (t + n_pre) + n_pre + (i % tiles) * tm) * N_HEADS, 0))
    return pl.pallas_call(
        _inproj_kernel,
        grid=(r // tm,),
        in_specs=[blk, pl.BlockSpec((1, d), lambda i: (0, 0)),
                  pl.BlockSpec(w_bf.shape, lambda i: (0, 0), pipeline_mode=pl.Buffered(1))],
        out_specs=[blk, blk, blk, blk3, blk3, blk, blk, blk, blk],
        out_shape=[bfo, bfo, bfo, h3o, h3o, f32o, f32o, f32o, f32o],
        compiler_params=_cparams(("parallel",)),
        name="inproj",
    )(x, g, w_bf)


def _fill_prefix_kernel(m_ref, buf_ref, o_ref):
    del buf_ref
    o_ref[...] = m_ref[...]


def _fill_prefix(m3, buf, n_seq):
    pre = m3.shape[0]
    stride = buf.shape[0] // n_seq
    assert stride % pre == 0
    return pl.pallas_call(
        _fill_prefix_kernel,
        grid=(n_seq,),
        in_specs=[pl.BlockSpec(m3.shape, lambda s: (0, 0)), pl.BlockSpec(memory_space=pl.ANY)],
        out_specs=pl.BlockSpec(m3.shape, lambda s: (s * (stride // pre), 0)),
        out_shape=jax.ShapeDtypeStruct(buf.shape, buf.dtype),
        input_output_aliases={1: 0},
        compiler_params=_cparams(("arbitrary",)),
        name="fill_prefix",
    )(m3, buf)


def _attn_prompt_kernel(lv_ref, q_ref, k_ref, v_ref, km_ref, vm_ref, g_ref, o_ref,
                        qq_scr, s_scr, p_scr, m_scr, l_scr, a_scr, acc_scr, *, tq, hp, rb):
    qi = pl.program_id(2)
    rows = 2 * tq
    nc = tq // LANES
    lam = _lam_from(lv_ref[...])
    lane = lax.broadcasted_iota(I32, (tq, HEAD_W), 1)
    heads = [slice(h * HEAD_W, (h + 1) * HEAD_W) for h in range(hp)]
    wide = lambda x: jnp.broadcast_to(x, (x.shape[0], LANES))

    def scores(qq, kc):
        return lax.dot_general(qq, kc, (((1,), (1,)), ((), ())), preferred_element_type=F32)

    for h, sl in enumerate(heads):
        q = q_ref[0, :, sl]
        zero = jnp.zeros_like(q)
        qq_scr[h, :tq, :] = jnp.where(lane < MAP_W, q, zero)
        qq_scr[h, tq:, :] = jnp.where(lane >= MAP_W, q, zero)
        s0 = scores(qq_scr[h], km_ref[:, sl])
        m = jnp.max(s0, axis=-1, keepdims=True)
        p0 = jnp.exp(s0 - m)
        m_scr[h] = wide(m)
        l_scr[h] = wide(jnp.sum(p0, axis=-1, keepdims=True))
        acc_scr[h] = jnp.dot(p0.astype(BF16), vm_ref[:, sl], preferred_element_type=F32)

    def key_tile(h, sl, off, causal):
        s_scr[h] = scores(qq_scr[h], k_ref[0, pl.ds(off, tq), sl])
        for r in range(rows // rb):
            rs = slice(r * rb, (r + 1) * rb)
            q0 = (r * rb) % tq
            live = min(nc, (q0 + rb - 1) // LANES + 1) if causal else nc
            cols = [s_scr[h, rs, c * LANES:(c + 1) * LANES] for c in range(live)]
            for c in range(live, nc):
                p_scr[h, rs, c * LANES:(c + 1) * LANES] = jnp.zeros((rb, LANES), BF16)
            if causal:
                qpos = lax.broadcasted_iota(I32, (rb, LANES), 0) + q0
                kpos = lax.broadcasted_iota(I32, (rb, LANES), 1)
                cols = [jnp.where(kpos + c * LANES <= qpos, s, -jnp.inf) for c, s in enumerate(cols)]
            mx = cols[0]
            for s in cols[1:]:
                mx = jnp.maximum(mx, s)
            m_prev = m_scr[h, rs, :]
            m_new = jnp.maximum(m_prev, jnp.max(mx, axis=-1, keepdims=True))
            alpha = jnp.exp(m_prev - m_new)
            psum = None
            for c, s in enumerate(cols):
                p = jnp.exp(s - m_new)
                psum = p if psum is None else psum + p
                p_scr[h, rs, c * LANES:(c + 1) * LANES] = p.astype(BF16)
            l_scr[h, rs, :] = alpha * l_scr[h, rs, :] + jnp.sum(psum, axis=-1, keepdims=True)
            m_scr[h, rs, :] = m_new
            a_scr[h, rs, :] = alpha
        acc_scr[h] = a_scr[h] * acc_scr[h] + jnp.dot(p_scr[h], v_ref[0, pl.ds(off, tq), sl],
                                                       preferred_element_type=F32)

    def body(j, c):
        off = pl.multiple_of(j * tq, tq)
        for h, sl in enumerate(heads):
            key_tile(h, sl, off, False)
        return c

    lax.fori_loop(0, qi, body, 0)

    off = pl.multiple_of(qi * tq, tq)
    for h, sl in enumerate(heads):
        key_tile(h, sl, off, True)
        o = acc_scr[h] / l_scr[h]
        o = o[:tq] - lam * o[tq:]
        o_ref[0, :, sl] = _subln(o, g_ref[...]).astype(o_ref.dtype)


def _attn_prompt(lv, qb, kb, vb, kmb, vmb, subln_g):
    b, t, d = qb.shape
    tq = _tile(t, ATT_TILE)
    hp = ATT_HEADS
    w = hp * HEAD_W
    assert tq % LANES == 0 and tq % ATT_STRIP == 0
    rows = 2 * tq
    kern = functools.partial(_attn_prompt_kernel, tq=tq, hp=hp, rb=ATT_STRIP)
    stat = pltpu.VMEM((hp, rows, LANES), F32)
    return pl.pallas_call(
        kern,
        grid=(b, N_HEADS // hp, t // tq),
        scratch_shapes=[pltpu.VMEM((hp, rows, HEAD_W), BF16), pltpu.VMEM((hp, rows, tq), F32),
                        pltpu.VMEM((hp, rows, tq), BF16), stat, stat, stat, stat],
        in_specs=[pl.BlockSpec(lv.shape, lambda bi, h, i: (0, 0)),
                  pl.BlockSpec((1, tq, w), lambda bi, h, i: (bi, i, h)),
                  pl.BlockSpec((1, t, w), lambda bi, h, i: (bi, 0, h)),
                  pl.BlockSpec((1, t, w), lambda bi, h, i: (bi, 0, h)),
                  pl.BlockSpec((N_META, w), lambda bi, h, i: (0, h)),
                  pl.BlockSpec((N_META, w), lambda bi, h, i: (0, h)),
                  pl.BlockSpec((1, HEAD_W), lambda bi, h, i: (0, 0))],
        out_specs=pl.BlockSpec((1, tq, w), lambda bi, h, i: (bi, i, h)),
        out_shape=jax.ShapeDtypeStruct((b, t, d), BF16),
        compiler_params=_cparams(("parallel", "parallel", "arbitrary")),
        name="attn_prompt",
    )(lv, qb, kb, vb, kmb, vmb, subln_g)


def _attn_sample_kernel(pt_ref, lv_ref, q_ref, kn_ref, vn_ref, g_ref, *rest, pp, s_new):
    kpages = rest[:pp]
    vpages = rest[pp:2 * pp]
    o_ref = rest[2 * pp]
    qq_scr, s_scr, p_scr, m_scr, l_scr, a_scr, acc_scr = rest[2 * pp + 1:]
    g = pl.program_id(1)
    ng = pl.num_programs(1)
    hr = 2 * s_new
    rows = N_HEADS * hr

    def head_bias(shape):
        row = lax.broadcasted_iota(I32, shape, 0)
        col = lax.broadcasted_iota(I32, shape, 1)
        return row, col, (col % N_HEADS) == (row // hr)

    @pl.when(g == 0)
    def _():
        lane = lax.broadcasted_iota(I32, (s_new, HEAD_W), 1)
        parts = []
        for h in range(N_HEADS):
            qh = q_ref[0, :, h * HEAD_W:(h + 1) * HEAD_W]
            parts += [jnp.where(lane < MAP_W, qh, 0.0), jnp.where(lane >= MAP_W, qh, 0.0)]
        qq_scr[...] = jnp.concatenate(parts, axis=0).astype(BF16)
        m_scr[...] = jnp.full(m_scr.shape, -jnp.inf, F32)
        l_scr[...] = jnp.zeros(l_scr.shape, F32)
        acc_scr[...] = jnp.zeros(acc_scr.shape, F32)

    def scores(kc):
        return lax.dot_general(qq_scr[...], kc, (((1,), (1,)), ((), ())), preferred_element_type=F32)

    def update(a, s_list, v_list):
        m_prev = m_scr[a]
        s_max = s_list[0]
        for s in s_list[1:]:
            s_max = jnp.maximum(s_max, s)
        m_new = jnp.maximum(m_prev, jnp.max(s_max, axis=-1, keepdims=True))
        alpha = jnp.exp(m_prev - m_new)
        l_add = None
        pv = None
        for s, vc in zip(s_list, v_list):
            p = jnp.exp(s - m_new[:, :1])
            ps = jnp.sum(p, axis=-1, keepdims=True)
            l_add = ps if l_add is None else l_add + ps
            d = jnp.dot(p.astype(BF16), vc, preferred_element_type=F32)
            pv = d if pv is None else pv + d
        m_scr[a] = m_new
        l_scr[a] = alpha * l_scr[a] + l_add
        acc_scr[a] = alpha * acc_scr[a] + pv

    def update_pages(a, pages):
        for i in pages:
            s_scr[i] = scores(kpages[i][...].astype(BF16))
        blocks = [(i, c) for i in pages for c in range(s_scr.shape[2] // LANES)]
        for r0 in range(0, rows, SOFTMAX_STRIP):
            rs = slice(r0, r0 + SOFTMAX_STRIP)
            head = (lax.broadcasted_iota(I32, (SOFTMAX_STRIP, LANES), 0) + r0) // hr
            lane = lax.broadcasted_iota(I32, (SOFTMAX_STRIP, LANES), 1)
            bias = jnp.where(lane % N_HEADS == head, 0.0, -jnp.inf)
            mx = None
            for i, c in blocks:
                s = s_scr[i, rs, c * LANES:(c + 1) * LANES] + bias
                mx = s if mx is None else jnp.maximum(mx, s)
            m_prev = m_scr[a, rs, :]
            m_new = jnp.maximum(m_prev, jnp.max(mx, axis=-1, keepdims=True))
            alpha = jnp.exp(m_prev - m_new)
            psum = None
            for i, c in blocks:
                p = jnp.exp(s_scr[i, rs, c * LANES:(c + 1) * LANES] + bias - m_new)
                psum = p if psum is None else psum + p
                p_scr[i, rs, c * LANES:(c + 1) * LANES] = p.astype(BF16)
            m_scr[a, rs, :] = m_new
            l_scr[a, rs, :] = alpha * l_scr[a, rs, :] + jnp.sum(psum, axis=-1, keepdims=True)
            a_scr[a, rs, :] = alpha
        pv = None
        for i in pages:
            d = jnp.dot(p_scr[i], vpages[i][...].astype(BF16), preferred_element_type=F32)
            pv = d if pv is None else pv + d
        acc_scr[a] = a_scr[a] * acc_scr[a] + pv

    n_acc = m_scr.shape[0]
    per = pp // n_acc
    for a in range(n_acc):
        update_pages(a, range(a * per, (a + 1) * per))

    @pl.when(g == ng - 1)
    def _():
        lam = _lam_from(lv_ref[...])
        row, col, same = head_bias((rows, s_new * N_HEADS))
        causal = (col // N_HEADS) <= ((row % hr) % s_new)
        s = jnp.where(same & causal, scores(kn_ref[0].astype(BF16)), -jnp.inf)
        update(0, [s], [vn_ref[0].astype(BF16)])
        m = m_scr[0]
        for a in range(1, n_acc):
            m = jnp.maximum(m, m_scr[a])
        l = jnp.zeros_like(m)
        acc = jnp.zeros(acc_scr.shape[1:], F32)
        for a in range(n_acc):
            w = jnp.exp(m_scr[a] - m)
            l = l + w * l_scr[a]
            acc = acc + w * acc_scr[a]
        o = acc / l
        for h in range(N_HEADS):
            oh = o[h * hr:h * hr + s_new] - lam * o[h * hr + s_new:(h + 1) * hr]
            o_ref[0, :, h * HEAD_W:(h + 1) * HEAD_W] = _subln(oh, g_ref[...])


def _attn_sample(page_table, lv, q, kn, vn, subln_g, cache_k, cache_v):
    bd, s_new, d = q.shape
    n_pages = page_table.shape[1]
    page_rows = cache_k.shape[1]
    pp = _tile(n_pages, PAGES_PER_STEP)
    n_acc = _tile(pp, SOFTMAX_SETS)
    rows = N_HEADS * 2 * s_new
    pt = page_table.reshape(-1)
    seq = lambda b, g, pt: (b, 0, 0)

    def page_spec(p):
        return pl.BlockSpec((None, page_rows, HEAD_W),
                            lambda b, g, pt, p=p: (pt[b * n_pages + g * pp + p], 0, 0))

    grid_spec = pltpu.PrefetchScalarGridSpec(
        num_scalar_prefetch=1,
        grid=(bd, n_pages // pp),
        in_specs=[pl.BlockSpec(lv.shape, lambda b, g, pt: (0, 0)),
                  pl.BlockSpec((1, s_new, d), seq),
                  pl.BlockSpec((1, s_new * N_HEADS, HEAD_W), seq),
                  pl.BlockSpec((1, s_new * N_HEADS, HEAD_W), seq),
                  pl.BlockSpec((1, HEAD_W), lambda b, g, pt: (0, 0))]
                 + [page_spec(p) for p in range(pp)] + [page_spec(p) for p in range(pp)],
        out_specs=pl.BlockSpec((1, s_new, d), seq),
        scratch_shapes=[pltpu.VMEM((rows, HEAD_W), BF16),
                        pltpu.VMEM((pp, rows, page_rows), F32),
                        pltpu.VMEM((pp, rows, page_rows), BF16),
                        pltpu.VMEM((n_acc, rows, LANES), F32),
                        pltpu.VMEM((n_acc, rows, LANES), F32),
                        pltpu.VMEM((n_acc, rows, LANES), F32),
                        pltpu.VMEM((n_acc, rows, HEAD_W), F32)],
    )
    kern = functools.partial(_attn_sample_kernel, pp=pp, s_new=s_new)
    return pl.pallas_call(
        kern,
        grid_spec=grid_spec,
        out_shape=jax.ShapeDtypeStruct((bd, s_new, d), F32),
        compiler_params=_cparams(("parallel", "arbitrary")),
        name="attn_sample",
    )(pt, lv, q, kn, vn, subln_g, *([cache_k] * pp), *([cache_v] * pp))


def _lru_kernel(xr_ref, yr_ref, cb0_ref, h0_ref, cw_ref, cb_ref, wa_ref, ba_ref, wx_ref, bx_ref,
                lam_ref, hg_ref, hl_ref, xbuf, a_scr, u_scr, h_scr, *, n_steps):
    gb, tt, w = xr_ref.shape
    bw = w // LRU_BLOCKS
    i = pl.program_id(1)
    halo = SUBLANES

    @pl.when(i == 0)
    def _():
        xbuf[:, halo - (CONV_W - 1):halo, :] = cb0_ref[...]
        h_scr[...] = h0_ref[...]

    xbuf[:, halo:halo + tt, :] = xr_ref[...]
    xc = cb_ref[...] + xbuf[:, halo - 3:halo - 3 + tt, :] * cw_ref[0:1, :]
    for j in range(1, CONV_W):
        xc = xc + xbuf[:, halo - 3 + j:halo - 3 + j + tt, :] * cw_ref[j:j + 1, :]
    xbuf[:, halo - (CONV_W - 1):halo, :] = xbuf[:, halo + tt - (CONV_W - 1):halo + tt, :]

    xc2 = xc.reshape(gb * tt, w)
    xcb = xc2.astype(BF16)
    r_parts, i_parts = [], []
    for c in range(LRU_BLOCKS):
        xs = xcb[:, c * bw:(c + 1) * bw]
        r_parts.append(jnp.dot(xs, wa_ref[c], preferred_element_type=F32))
        i_parts.append(jnp.dot(xs, wx_ref[c], preferred_element_type=F32))
    r = _sigmoid(jnp.concatenate(r_parts, axis=-1) + ba_ref[...])
    ig = _sigmoid(jnp.concatenate(i_parts, axis=-1) + bx_ref[...])
    nl = -lam_ref[...]
    softplus = jnp.maximum(nl, 0.0) + jnp.log1p(jnp.exp(-jnp.abs(nl)))
    log_a = (-LRU_C * r) * softplus
    a_scr[...] = jnp.exp(log_a).reshape(gb, tt, w)
    mult = jnp.sqrt(-_expm1(2.0 * log_a))
    u_scr[...] = (xc2 * ig * mult).reshape(gb, tt, w)

    def step(t, h):
        h = a_scr[:, t, :] * h + u_scr[:, t, :]
        u_scr[:, t, :] = h
        return h

    h = lax.fori_loop(0, n_steps, step, h_scr[...])
    h_scr[...] = h
    hl_ref[...] = h
    y = yr_ref[...]
    gelu = y * (0.5 * (1.0 + jnp.tanh(math.sqrt(2.0 / math.pi) * (y + 0.044715 * (y * y * y)))))
    hg_ref[...] = (u_scr[...] * gelu).astype(hg_ref.dtype)


def _lru(xr, yr, cb0, h0, p, tt, n_steps):
    b, t, w = xr.shape
    gb = LRU_GB
    assert b % gb == 0 and t % tt == 0 and (n_steps == tt or t == tt)
    bw = w // LRU_BLOCKS
    blk = pl.BlockSpec((gb, tt, w), lambda g, i: (g, i, 0))
    vec = pl.BlockSpec((1, w), lambda g, i: (0, 0))
    wsp = pl.BlockSpec((LRU_BLOCKS, bw, bw), lambda g, i: (0, 0, 0))
    kern = functools.partial(_lru_kernel, n_steps=n_steps)
    return pl.pallas_call(
        kern,
        grid=(b // gb, t // tt),
        in_specs=[blk, blk,
                  pl.BlockSpec((gb, CONV_W - 1, w), lambda g, i: (g, 0, 0)),
                  pl.BlockSpec((gb, w), lambda g, i: (g, 0)),
                  pl.BlockSpec((CONV_W, w), lambda g, i: (0, 0)), vec, wsp, vec, wsp, vec, vec],
        out_specs=[blk, pl.BlockSpec((gb, w), lambda g, i: (g, 0))],
        out_shape=[jax.ShapeDtypeStruct((b, t, w), BF16), jax.ShapeDtypeStruct((b, w), F32)],
        scratch_shapes=[pltpu.VMEM((gb, tt + SUBLANES, w), F32), pltpu.VMEM((gb, tt, w), F32),
                        pltpu.VMEM((gb, tt, w), F32), pltpu.VMEM((gb, w), F32)],
        compiler_params=_cparams(("parallel", "arbitrary")),
        name="lru",
    )(xr, yr, cb0, h0, p["conv_w"], p["conv_b"], p["wa"], p["ba"], p["wx"], p["bx"], p["lam"])


def _post_kernel(o_ref, hg_ref, ga_ref, gr_ref, x_ref, wap_ref, wlp_ref, wo_ref, g2_ref, rw_ref,
                 rb_ref, cin_ref, h1_ref, xn_ref, ei_ref, gt_ref, rk_ref, cnt_ref, carry):
    tm = x_ref.shape[0]

    @pl.when(pl.program_id(0) == 0)
    def _():
        carry[...] = cin_ref[...]

    attn_out = jnp.dot(o_ref[...], wap_ref[...], preferred_element_type=F32)
    lru_out = jnp.dot(hg_ref[...], wlp_ref[...], preferred_element_type=F32)
    merged = _sigmoid(ga_ref[...]) * attn_out + _sigmoid(gr_ref[...]) * lru_out
    h1 = x_ref[...] + jnp.dot(merged.astype(BF16), wo_ref[...], preferred_element_type=F32)
    h1_ref[...] = h1
    ms = jnp.mean(h1 * h1, axis=-1, keepdims=True)
    xn = (h1 * lax.rsqrt(ms + EPS)) * g2_ref[...]
    _store_row_tiles(xn_ref, xn)

    lane = lax.broadcasted_iota(I32, (tm, LANES), 1)
    logits = jnp.dot(xn.astype(BF16), rw_ref[...], preferred_element_type=F32) + rb_ref[...]
    lg = jnp.where(lane < N_EXPERTS, logits, NEG_BIG)
    tops, idxs, hots = [], [], []
    for _ in range(TOP_K):
        mx = jnp.max(lg, axis=-1, keepdims=True)
        idx = jnp.min(jnp.where(lg == mx, lane, LANES), axis=-1, keepdims=True)
        hot = lane == idx
        tops.append(mx); idxs.append(idx); hots.append(hot)
        lg = jnp.where(hot, NEG_BIG, lg)
    exps = [jnp.exp(tv - tops[0]) for tv in tops]
    den = exps[0] + exps[1] + exps[2] + exps[3]

    sel = jnp.zeros((tm, LANES), F32)
    for hot in hots:
        sel = sel + jnp.where(hot, 1.0, 0.0)
    r_i = lax.broadcasted_iota(I32, (tm, tm), 0)
    c_i = lax.broadcasted_iota(I32, (tm, tm), 1)
    ltri = jnp.where(c_i < r_i, 1.0, 0.0).astype(BF16)
    base = jnp.dot(ltri, sel.astype(BF16), preferred_element_type=F32) + carry[...]
    new_carry = carry[...] + jnp.sum(sel, axis=0, keepdims=True)
    carry[...] = new_carry
    cnt_ref[...] = new_carry

    ei = jnp.zeros((tm, LANES), I32)
    gt = jnp.zeros((tm, LANES), F32)
    rk = jnp.zeros((tm, LANES), I32)
    for k in range(TOP_K):
        rank = jnp.sum(jnp.where(hots[k], base, 0.0), axis=-1, keepdims=True)
        ei = jnp.where(lane == k, idxs[k], ei)
        gt = jnp.where(lane == k, exps[k] / den, gt)
        rk = jnp.where(lane == k, rank.astype(I32), rk)
    ei_ref[...] = ei
    gt_ref[...] = gt
    rk_ref[...] = rk


def _post(o, hg, ga, gr, x, p, cnt_in, tm):
    r, d = x.shape
    assert r % tm == 0
    row = lambda i: (i, 0)
    fix = lambda i: (0, 0)
    blk = pl.BlockSpec((tm, d), row)
    wsp = pl.BlockSpec((d, d), fix, pipeline_mode=pl.Buffered(1))
    lsp = pl.BlockSpec((tm, LANES), row)
    vec = pl.BlockSpec((1, LANES), fix)
    return pl.pallas_call(
        _post_kernel,
        grid=(r // tm,),
        in_specs=[blk, blk, blk, blk, blk, wsp, wsp, wsp, pl.BlockSpec((1, d), fix),
                  pl.BlockSpec((d, LANES), fix), vec, vec],
        out_specs=[blk, pl.BlockSpec((tm * SUBLANES, LANES), row), lsp, lsp, lsp, vec],
        out_shape=[jax.ShapeDtypeStruct((r, d), F32), jax.ShapeDtypeStruct((r * SUBLANES, LANES), F32),
                   jax.ShapeDtypeStruct((r, LANES), I32), jax.ShapeDtypeStruct((r, LANES), F32),
                   jax.ShapeDtypeStruct((r, LANES), I32), jax.ShapeDtypeStruct((1, LANES), F32)],
        scratch_shapes=[pltpu.VMEM((1, LANES), F32)],
        compiler_params=_cparams(("arbitrary",)),
        name="post_router",
    )(o, hg, ga, gr, x, p["w_attn_proj"], p["w_lru_proj"], p["w_out"], p["norm2_g"],
      p["router_w"], p["router_b"], cnt_in)


def _pad_rows_kernel(ps_ref, cnt_ref, rows_ref, zbuf, sem, *, tmx):
    zbuf[...] = jnp.zeros(zbuf.shape, F32)
    sizes = [1 << b for b in reversed(range((tmx - 1).bit_length()))]

    def visit(e, wait):
        cnt = cnt_ref[e]
        start = ps_ref[e] + cnt
        pad = lax.rem(tmx - lax.rem(cnt, tmx), tmx)
        for size in sizes:
            part = pad & size

            @pl.when(part != 0)
            def _():
                cp = pltpu.make_async_copy(zbuf.at[pl.ds(0, size)], rows_ref.at[pl.ds(start, size)], sem)
                cp.wait() if wait else cp.start()

            start = start + part

    def starts(e, c):
        visit(e, False)
        return c

    def waits(e, c):
        visit(e, True)
        return c

    lax.fori_loop(0, N_EXPERTS, starts, 0)
    lax.fori_loop(0, N_EXPERTS, waits, 0)


def _pad_rows(pstart, counts, n_rows, tmx):
    smem = pl.BlockSpec(memory_space=pltpu.SMEM)
    return pl.pallas_call(
        functools.partial(_pad_rows_kernel, tmx=tmx),
        in_specs=[smem, smem],
        out_specs=pl.BlockSpec(memory_space=pl.ANY),
        out_shape=jax.ShapeDtypeStruct((n_rows, SUBLANES, LANES), F32),
        scratch_shapes=[pltpu.VMEM((tmx // 2, SUBLANES, LANES), F32), pltpu.SemaphoreType.DMA(())],
        name="moe_pad_rows",
    )(pstart, counts)


def _dispatch_kernel(e_ref, rk_ref, ps_ref, xn_ref, rows_in_ref, rows_ref, sem, *, tm):
    del rows_in_ref

    def row_copy(j, d):
        return pltpu.make_async_copy(xn_ref.at[j], rows_ref.at[d], sem)

    def issue(jj, c):
        for u in range(DMA_UNROLL):
            j = jj * DMA_UNROLL + u
            for k in range(TOP_K):
                n = j * TOP_K + k
                row_copy(j, ps_ref[e_ref[n]] + rk_ref[n]).start()
        return c

    lax.fori_loop(0, tm // DMA_UNROLL, issue, 0)
    for _ in range(TOP_K):
        pltpu.make_async_copy(xn_ref, rows_ref.at[pl.ds(0, tm)], sem).wait()


def _dispatch(e_flat, rk_flat, pstart, xn, rows, tm):
    n = xn.shape[0]
    assert n % tm == 0
    smem = lambda shape, imap: pl.BlockSpec(shape, imap, memory_space=pltpu.SMEM)
    kern = functools.partial(_dispatch_kernel, tm=tm)
    return pl.pallas_call(
        kern,
        grid=(n // tm,),
        in_specs=[smem((tm * TOP_K,), lambda i: (i,)), smem((tm * TOP_K,), lambda i: (i,)),
                  smem(pstart.shape, lambda i: (0,)),
                  pl.BlockSpec((tm, SUBLANES, LANES), lambda i: (i, 0, 0)), pl.BlockSpec(memory_space=pl.ANY)],
        out_specs=pl.BlockSpec(memory_space=pl.ANY),
        out_shape=jax.ShapeDtypeStruct(rows.shape, rows.dtype),
        scratch_shapes=[pltpu.SemaphoreType.DMA(())],
        input_output_aliases={4: 0},
        compiler_params=_cparams(("arbitrary",)),
        name="moe_dispatch",
    )(e_flat, rk_flat, pstart, xn, rows)


def _expert_kernel(be_ref, nu_ref, x_ref, wg_ref, bg_ref, wu_ref, bu_ref, wd_ref, bd_ref, y_ref,
                   wg_s, wu_s, wd_s):
    j = pl.program_id(0)
    prev = be_ref[jnp.maximum(j - 1, 0)]

    @pl.when((j == 0) | (be_ref[j] != prev))
    def _():
        wg_s[...] = wg_ref[0].astype(BF16)
        wu_s[...] = wu_ref[0].astype(BF16)
        wd_s[...] = wd_ref[0].astype(BF16)

    @pl.when(j < nu_ref[0])
    def _():
        tm = x_ref.shape[0] // SUBLANES
        xb = _load_row_tiles(x_ref, tm).astype(BF16)
        g = jnp.dot(xb, wg_s[...], preferred_element_type=F32) + bg_ref[0]
        up = jnp.dot(xb, wu_s[...], preferred_element_type=F32) + bu_ref[0]
        g = jnp.minimum(g, SWIGLU_LIMIT)
        up = jnp.clip(up, -SWIGLU_LIMIT, SWIGLU_LIMIT)
        h = (up + 1.0) * (g * _sigmoid(SWIGLU_ALPHA * g))
        _store_row_tiles(y_ref, jnp.dot(h.astype(BF16), wd_s[...], preferred_element_type=F32) + bd_ref[0])


def _experts(blk_e, n_used, rows, p, tm):
    n_rows = rows.shape[0] // SUBLANES
    d = SUBLANES * LANES
    dff = p["w_gate"].shape[2]
    n_blk = n_rows // tm
    wsp = lambda shape: pl.BlockSpec((1,) + shape, lambda j, be, nu: (be[j], 0, 0))
    grid_spec = pltpu.PrefetchScalarGridSpec(
        num_scalar_prefetch=2,
        grid=(n_blk,),
        in_specs=[pl.BlockSpec((tm * SUBLANES, LANES), lambda j, be, nu: (j, 0)),
                  wsp((d, dff)), wsp((1, dff)), wsp((d, dff)), wsp((1, dff)),
                  wsp((dff, d)), wsp((1, d))],
        out_specs=pl.BlockSpec((tm * SUBLANES, LANES), lambda j, be, nu: (j, 0)),
        scratch_shapes=[pltpu.VMEM((d, dff), BF16), pltpu.VMEM((d, dff), BF16),
                        pltpu.VMEM((dff, d), BF16)],
    )
    return pl.pallas_call(
        _expert_kernel,
        grid_spec=grid_spec,
        out_shape=jax.ShapeDtypeStruct((n_rows * SUBLANES, LANES), F32),
        compiler_params=_cparams(("arbitrary",)),
        name="moe_experts",
    )(blk_e, n_used, rows, p["w_gate"], p["b_gate"], p["w_up"], p["b_up"], p["w_down"], p["b_down"])


def _combine_kernel(e_ref, rk_ref, en_ref, rkn_ref, ps_ref, y_ref, gt_ref, h1_ref, gf_ref, out_ref,
                    buf, sems, *, tm):
    i = pl.program_id(0)
    slot = i % 2

    def gather(er, rr, sl):
        def issue(jj, c):
            for u in range(DMA_UNROLL):
                j = jj * DMA_UNROLL + u
                for k in range(TOP_K):
                    n = j * TOP_K + k
                    dst = buf.at[sl, k, pl.ds(pl.multiple_of(j * SUBLANES, SUBLANES), SUBLANES), :]
                    pltpu.make_async_copy(y_ref.at[ps_ref[er[n]] + rr[n]], dst, sems.at[sl]).start()
            return c

        lax.fori_loop(0, tm // DMA_UNROLL, issue, 0)

    @pl.when(i == 0)
    def _():
        gather(e_ref, rk_ref, 0)

    @pl.when(i + 1 < pl.num_programs(0))
    def _():
        gather(en_ref, rkn_ref, 1 - slot)

    def finish(sl):
        for k in range(TOP_K):
            pltpu.make_async_copy(buf.at[sl, k], buf.at[sl, k], sems.at[sl]).wait()
        gt = gt_ref[...]
        y = _load_row_tiles(buf.at[sl, 0], tm) * gt[:, 0:1]
        for k in range(1, TOP_K):
            y = y + _load_row_tiles(buf.at[sl, k], tm) * gt[:, k:k + 1]
        h2 = h1_ref[...] + y
        ms = jnp.mean(h2 * h2, axis=-1, keepdims=True)
        out_ref[...] = (h2 * lax.rsqrt(ms + EPS)) * gf_ref[...]

    for sl in range(2):
        pl.when(slot == sl)(functools.partial(finish, sl))


def _combine(e_flat, rk_flat, pstart, y_rows, gt, h1, gf, tm):
    n, d = h1.shape
    assert n % tm == 0
    nt = n // tm
    smem = lambda shape, imap: pl.BlockSpec(shape, imap, memory_space=pltpu.SMEM)
    row = lambda i: (i, 0)
    cur = lambda i: (i,)
    nxt = lambda i: (jnp.minimum(i + 1, nt - 1),)
    kern = functools.partial(_combine_kernel, tm=tm)
    return pl.pallas_call(
        kern,
        grid=(nt,),
        in_specs=[smem((tm * TOP_K,), cur), smem((tm * TOP_K,), cur),
                  smem((tm * TOP_K,), nxt), smem((tm * TOP_K,), nxt),
                  smem(pstart.shape, lambda i: (0,)),
                  pl.BlockSpec(memory_space=pl.ANY),
                  pl.BlockSpec((tm, LANES), row), pl.BlockSpec((tm, d), row),
                  pl.BlockSpec((1, d), lambda i: (0, 0))],
        out_specs=pl.BlockSpec((tm, d), row),
        out_shape=jax.ShapeDtypeStruct((n, d), F32),
        scratch_shapes=[pltpu.VMEM((2, TOP_K, tm * SUBLANES, LANES), F32), pltpu.SemaphoreType.DMA((2,))],
        compiler_params=_cparams(("arbitrary",)),
        name="moe_combine",
    )(e_flat, rk_flat, e_flat, rk_flat, pstart, y_rows, gt, h1, gf)


def kernel(x_prompt, x_sample, cache_k, cache_v, state_conv, state_h, page_table, meta_tokens, norm1_g, w_in, lambda_q1, lambda_k1, lambda_q2, lambda_k2, subln_g, conv_w, conv_b, rg_wa, rg_ba, rg_wx, rg_bx, rg_lambda, w_attn_proj, w_lru_proj, w_out, norm2_g, router_w, router_b, w_gate, b_gate, w_up, b_up, w_down, b_down, final_norm_g):
    depth = w_in.shape[0]
    assert depth == 1, "single-layer stack"
    b, t, d = x_prompt.shape
    bd, s_new, _ = x_sample.shape
    n_p, n_s = b * t, bd * s_new
    row1 = lambda a: a.reshape(1, -1)

    w_in_bf = w_in[0].astype(BF16)
    g1 = row1(norm1_g[0])
    lv = jnp.stack([lambda_q1[0], lambda_k1[0], lambda_q2[0], lambda_k2[0]])
    sg = row1(subln_g[0])
    lru_p = dict(conv_w=conv_w[0], conv_b=row1(conv_b[0]), wa=rg_wa[0].astype(BF16), ba=row1(rg_ba[0]),
                 wx=rg_wx[0].astype(BF16), bx=row1(rg_bx[0]), lam=row1(rg_lambda[0]))
    rw = jnp.zeros((d, LANES), BF16).at[:, :N_EXPERTS].set(router_w[0].astype(BF16))
    rb = jnp.zeros((1, LANES), F32).at[0, :N_EXPERTS].set(router_b[0])
    post_p = dict(w_attn_proj=w_attn_proj[0].astype(BF16), w_lru_proj=w_lru_proj[0].astype(BF16),
                  w_out=w_out[0].astype(BF16), norm2_g=row1(norm2_g[0]), router_w=rw, router_b=rb)
    exp_p = dict(w_gate=w_gate[0], b_gate=b_gate[0][:, None, :], w_up=w_up[0], b_up=b_up[0][:, None, :],
                 w_down=w_down[0], b_down=b_down[0][:, None, :])

    xp = x_prompt.reshape(n_p, d)
    xs = x_sample.reshape(n_s, d)
    qbP, kbP, vbP, k3P, v3P, xrP, yrP, gaP, grP = _inproj(xp, g1, w_in_bf, _tile(t, ROW_TILE), (t, N_META))
    qbS, kbS, vbS, k3S, v3S, xrS, yrS, gaS, grS = _inproj(xs, g1, w_in_bf, _tile(n_s, ROW_TILE))
    _, kbM, vbM, k3M, v3M, xrM, yrM, _, _ = _inproj(meta_tokens, g1, w_in_bf, N_META)

    oP = _attn_prompt(lv, qbP.reshape(b, t, d), kbP.reshape(b, t, d), vbP.reshape(b, t, d), kbM, vbM, sg)
    n_pool, page = cache_k.shape[1], cache_k.shape[2]
    oS = _attn_sample(page_table, lv, qbS.astype(F32).reshape(bd, s_new, d),
                      k3S.reshape(bd, s_new * N_HEADS, HEAD_W), v3S.reshape(bd, s_new * N_HEADS, HEAD_W), sg,
                      cache_k[0].reshape(n_pool, page * N_HEADS, HEAD_W),
                      cache_v[0].reshape(n_pool, page * N_HEADS, HEAD_W))

    gb = LRU_GB
    zc = jnp.zeros((gb, CONV_W - 1, d), F32)
    zh = jnp.zeros((gb, d), F32)
    bc = lambda a: jnp.broadcast_to(a[None], (gb,) + a.shape)
    _, hM = _lru(bc(xrM), bc(yrM), zc, zh, lru_p, N_META, N_META)
    cbP = jnp.broadcast_to(xrM[None, N_META - (CONV_W - 1):], (b, CONV_W - 1, d))
    h0P = jnp.broadcast_to(hM[0:1], (b, d))
    xrP3 = xrP.reshape(b, t, d)
    hgP, hlP = _lru(xrP3, yrP.reshape(b, t, d), cbP, h0P, lru_p, _tile(t, LRU_TT), _tile(t, LRU_TT))
    t_pad = SUBLANES
    padt = lambda a: jnp.pad(a.reshape(bd, s_new, d), ((0, 0), (0, t_pad - s_new), (0, 0)))
    xrS3 = xrS.reshape(bd, s_new, d)
    hgS, hlS = _lru(padt(xrS), padt(yrS), state_conv[0], state_h[0], lru_p, t_pad, s_new)
    hgS = hgS[:, :s_new].reshape(n_s, d)

    cnt0 = jnp.zeros((1, LANES), F32)
    h1P, xnP, eiP, gtP, rkP, cntP = _post(oP.reshape(n_p, d), hgP.reshape(n_p, d), gaP, grP, xp, post_p,
                                          cnt0, _tile(n_p, POST_TILE))
    h1S, xnS, eiS, gtS, rkS, cnt = _post(oS.reshape(n_s, d).astype(BF16), hgS, gaS, grS, xs, post_p,
                                         cntP, _tile(n_s, ROW_TILE))

    n_tok = n_p + n_s
    tmx = MOE_TILE
    counts = cnt[0, :N_EXPERTS].astype(I32)
    padded = (counts + tmx - 1) // tmx * tmx
    pend = jnp.cumsum(padded)
    pstart = (pend - padded).astype(I32)
    n_rows = -(-(n_tok * TOP_K + N_EXPERTS * (tmx - 1)) // tmx) * tmx
    n_blk = n_rows // tmx
    blk_start = jnp.arange(n_blk, dtype=I32) * tmx
    blk_e = jnp.minimum(jnp.sum((pend[None, :] <= blk_start[:, None]).astype(I32), axis=1), N_EXPERTS - 1)
    n_used = (pend[-1:] // tmx).astype(I32)
    flat = lambda a: a[:, :TOP_K].reshape(-1)

    tiles = lambda a: a.reshape(-1, SUBLANES, LANES)
    rows = _pad_rows(pstart, counts, n_rows, tmx)
    rows = _dispatch(flat(eiP), flat(rkP), pstart, tiles(xnP), rows, _tile(n_p, DISPATCH_TILE))
    rows = _dispatch(flat(eiS), flat(rkS), pstart, tiles(xnS), rows, _tile(n_s, DISPATCH_TILE))
    y_rows = tiles(_experts(blk_e, n_used, rows.reshape(-1, LANES), exp_p, tmx))
    gf = row1(final_norm_g)
    yP = _combine(flat(eiP), flat(rkP), pstart, y_rows, gtP, h1P, gf, _tile(n_p, COMBINE_TILE))
    yS = _combine(flat(eiS), flat(rkS), pstart, y_rows, gtS, h1S, gf, _tile(n_s, COMBINE_TILE))

    def with_meta(x3, m3):
        return _fill_prefix(m3, x3, b).reshape(1, b, t + N_META, N_HEADS, HEAD_W)

    nc = CONV_W - 1
    return (yP.reshape(b, t, d), yS.reshape(bd, s_new, d),
            with_meta(k3P, k3M), with_meta(v3P, v3M),
            xrP3[:, t - nc:][None], hlP[None],
            k3S.reshape(1, bd, s_new, N_HEADS, HEAD_W), v3S.reshape(1, bd, s_new, N_HEADS, HEAD_W),
            xrS3[:, s_new - nc:][None], hlS[None])
```
